```python
import jax, jax.numpy as jnp
from jax import lax
import numpy as np

D_MODEL = 1024
BATCH = 8
SEQ = 4096
DEPTH = 1

CHUNK = 64
LEFT_CHUNKS = 8
BAND = (LEFT_CHUNKS + 1) * CHUNK
HEAD_DIM = 64
RWKV_HEADS = 8
ATTN_HEADS = 8
RWKV_WIDTH = RWKV_HEADS * HEAD_DIM
ATTN_WIDTH = ATTN_HEADS * HEAD_DIM
DECAY_RANK = 64
ICLR_RANK = 64
GATE_RANK = 128
RWKV_COLS = 3 * RWKV_WIDTH + DECAY_RANK + ICLR_RANK + GATE_RANK
ATTN_COLS = 3 * ATTN_WIDTH
GATE_COLS = 2 * D_MODEL
IN_COLS = RWKV_COLS + ATTN_COLS + GATE_COLS
MAX_REL = 128
REL_TABLE = (CHUNK - 1) + MAX_REL + 1
D_FF = 2816
CONV_W = 3
PLE_DIM = 256
NORM_EPS = 1e-6
GN_EPS = 64e-5
NEG_INF = -1e30

kernel_name = "hybrid_rwkv7_chunkattn_convglu_block"


def rms_norm(x, g):
    xf = x.astype(jnp.float32)
    y = xf * lax.rsqrt(jnp.mean(xf * xf, axis=-1, keepdims=True) + NORM_EPS)
    return (y * g.astype(jnp.float32)).astype(x.dtype)


def token_shift(z):
    return jnp.pad(z, ((0, 0), (1, 0), (0, 0)))[:, :-1]


def rwkv7_time_mix(z, mu, w0, w2, a0, a2, g2, k_k, k_a, r_k, lnx_w, lnx_b):
    B, T, _ = z.shape
    H, N = RWKV_HEADS, HEAD_DIM
    zs = z + (token_shift(z) - z) * mu
    W = RWKV_WIDTH
    r, k, v, wd, ad, gd = jnp.split(
        zs, [W, 2 * W, 3 * W, 3 * W + DECAY_RANK, 3 * W + DECAY_RANK + ICLR_RANK], axis=-1)
    f32 = jnp.float32
    w_log = -jax.nn.softplus(-(w0 + jnp.tanh(wd) @ w2).astype(f32)) - 0.5
    decay = jnp.exp(-jnp.exp(w_log))
    a = jax.nn.sigmoid((a0 + ad @ a2).astype(f32))
    g = jax.nn.sigmoid(gd) @ g2
    r, k, v = r.astype(f32), k.astype(f32), v.astype(f32)
    kk = (k * k_k.astype(f32)).reshape(B, T, H, N)
    kk = kk / jnp.maximum(jnp.sqrt(jnp.sum(kk * kk, axis=-1, keepdims=True)), 1e-12)
    k = k * (1.0 + (a - 1.0) * k_a.astype(f32))
    heads = lambda t: t.reshape(B, T, H, N)
    r, k, v, a, decay = heads(r), heads(k), heads(v), heads(a), heads(decay)
    tm = lambda t: jnp.moveaxis(t, 1, 0)

    def step(S, inp):
        r_t, d_t, k_t, v_t, kk_t, a_t = inp
        sa = jnp.einsum('bhij,bhj->bhi', S, -kk_t)
        S = (S * d_t[:, :, None, :]
             + sa[..., :, None] * (kk_t * a_t)[..., None, :]
             + v_t[..., :, None] * k_t[..., None, :])
        y_t = jnp.einsum('bhij,bhj->bhi', S, r_t)
        return S, y_t

    S0 = jnp.zeros((B, H, N, N), f32)
    _, y = lax.scan(step, S0, (tm(r), tm(decay), tm(k), tm(v), tm(kk), tm(a)))
    y = jnp.moveaxis(y, 0, 1)
    mean = jnp.mean(y, axis=-1, keepdims=True)
    var = jnp.mean(jnp.square(y - mean), axis=-1, keepdims=True)
    y = (y - mean) * lax.rsqrt(var + GN_EPS)
    y = y * lnx_w.astype(f32).reshape(H, N) + lnx_b.astype(f32).reshape(H, N)
    bonus = jnp.sum(r * k * r_k.astype(f32), axis=-1, keepdims=True) * v
    y = (y + bonus).reshape(B, T, W)
    return (y * g.astype(f32)).astype(z.dtype)


def chunk_attention(q, k, v, rel_bias):
    B, T, H, Dh = q.shape
    NC = T // CHUNK
    pad = LEFT_CHUNKS * CHUNK
    kp = jnp.pad(k, ((0, 0), (pad, 0), (0, 0), (0, 0)))
    vp = jnp.pad(v, ((0, 0), (pad, 0), (0, 0), (0, 0)))
    qc = jnp.moveaxis(q.reshape(B, NC, CHUNK, H, Dh), 1, 0)
    q_off = jnp.arange(CHUNK)
    k_off = jnp.arange(BAND) - pad
    rel = q_off[:, None] - k_off[None, :]
    idx = jnp.clip(rel, -(CHUNK - 1), MAX_REL) + (CHUNK - 1)
    bias = rel_bias.astype(jnp.float32)[:, idx]
    scale = HEAD_DIM ** -0.5

    def one_chunk(args):
        c, qb = args
        start = c * CHUNK
        kb = lax.dynamic_slice_in_dim(kp, start, BAND, axis=1)
        vb = lax.dynamic_slice_in_dim(vp, start, BAND, axis=1)
        s = jnp.einsum('bqhd,bkhd->bhqk', qb, kb).astype(jnp.float32) * scale + bias
        valid = (k_off + start) >= 0
        s = jnp.where(valid[None, None, None, :], s, NEG_INF)
        pr = jax.nn.softmax(s, axis=-1).astype(vb.dtype)
        return jnp.einsum('bhqk,bkhd->bqhd', pr, vb)

    o = lax.map(one_chunk, (jnp.arange(NC), qc))
    return jnp.moveaxis(o, 0, 1).reshape(B, T, H * Dh)


def causal_dwconv(x, w, b):
    y = lax.conv_general_dilated(
        x, w[:, None, :].astype(x.dtype), window_strides=(1,), padding=[(CONV_W - 1, 0)],
        dimension_numbers=('NWC', 'WIO', 'NWC'), feature_group_count=x.shape[-1])
    return y + b


def conv_glu_ffn(h, w_up, cw, cb, w_down):
    u = h @ w_up
    a, gv = jnp.split(u, 2, axis=-1)
    a = causal_dwconv(a, cw, cb)
    return (jax.nn.gelu(a, approximate=False) * gv) @ w_down


def setup_inputs(seed: int = 0) -> dict:
    key = jax.random.key(seed)
    ks = iter(jax.random.split(key, 40))
    f32 = jnp.float32
    nrm = lambda shape, s: jax.random.normal(next(ks), shape, f32) * s
    ones = lambda shape, s: 1.0 + jax.random.normal(next(ks), shape, f32) * s
    L = DEPTH
    return {
        "x": nrm((BATCH, SEQ, D_MODEL), 1.0),
        "p": nrm((L, BATCH, SEQ, PLE_DIM), 1.0),
        "ln1_g": ones((L, D_MODEL), 0.01),
        "w_in": nrm((L, D_MODEL, IN_COLS), D_MODEL ** -0.5),
        "mix_mu": jax.random.uniform(next(ks), (L, RWKV_COLS), f32, 0.0, 1.0),
        "w0": nrm((L, RWKV_WIDTH), 0.5),
        "w2": nrm((L, DECAY_RANK, RWKV_WIDTH), DECAY_RANK ** -0.5),
        "a0": nrm((L, RWKV_WIDTH), 0.1),
        "a2": nrm((L, ICLR_RANK, RWKV_WIDTH), ICLR_RANK ** -0.5),
        "g2": nrm((L, GATE_RANK, RWKV_WIDTH), GATE_RANK ** -0.5),
        "k_k": ones((L, RWKV_WIDTH), 0.1),
        "k_a": ones((L, RWKV_WIDTH), 0.1),
        "r_k": nrm((L, RWKV_HEADS, HEAD_DIM), 0.1),
        "lnx_w": ones((L, RWKV_WIDTH), 0.01),
        "lnx_b": nrm((L, RWKV_WIDTH), 0.01),
        "rel_bias": nrm((L, ATTN_HEADS, REL_TABLE), 0.1),
        "gate_b": nrm((L, GATE_COLS), 0.01),
        "w_br_rwkv": nrm((L, RWKV_WIDTH, D_MODEL), RWKV_WIDTH ** -0.5),
        "w_br_attn": nrm((L, ATTN_WIDTH, D_MODEL), ATTN_WIDTH ** -0.5),
        "w_o": nrm((L, D_MODEL, D_MODEL), D_MODEL ** -0.5),
        "ln2_g": ones((L, D_MODEL), 0.01),
        "w_ffn_up": nrm((L, D_MODEL, 2 * D_FF), D_MODEL ** -0.5),
        "conv_w": nrm((L, CONV_W, D_FF), CONV_W ** -0.5),
        "conv_b": nrm((L, D_FF), 0.01),
        "w_ffn_down": nrm((L, D_FF, D_MODEL), D_FF ** -0.5),
        "ln3_g": ones((L, D_MODEL), 0.01),
        "w_ple": nrm((L, PLE_DIM, D_MODEL), PLE_DIM ** -0.5),
        "w_pg": nrm((L, D_MODEL, D_MODEL), D_MODEL ** -0.5),
        "b_pg": nrm((L, D_MODEL), 0.01),
        "lnf_g": ones((D_MODEL,), 0.01),
    }


def reference(x, p, ln1_g, w_in, mix_mu, w0, w2, a0, a2, g2, k_k, k_a, r_k, lnx_w, lnx_b,
              rel_bias, gate_b, w_br_rwkv, w_br_attn, w_o, ln2_g, w_ffn_up, conv_w, conv_b,
              w_ffn_down, ln3_g, w_ple, w_pg, b_pg, lnf_g):
    B, T, _ = x.shape
    for i in range(DEPTH):
        h = rms_norm(x, ln1_g[i])
        z = h @ w_in[i]
        z_rwkv, z_q, z_k, z_v, z_gate = jnp.split(
            z, [RWKV_COLS, RWKV_COLS + ATTN_WIDTH, RWKV_COLS + 2 * ATTN_WIDTH,
                RWKV_COLS + ATTN_COLS], axis=-1)
        o_a = rwkv7_time_mix(z_rwkv, mix_mu[i], w0[i], w2[i], a0[i], a2[i], g2[i],
                             k_k[i], k_a[i], r_k[i], lnx_w[i], lnx_b[i])
        heads = lambda t: t.reshape(B, T, ATTN_HEADS, HEAD_DIM)
        o_b = chunk_attention(heads(z_q), heads(z_k), heads(z_v), rel_bias[i])
        gates = jax.nn.sigmoid(z_gate + gate_b[i])
        g_a, g_b = jnp.split(gates, 2, axis=-1)
        merged = g_a * (o_a @ w_br_rwkv[i]) + g_b * (o_b @ w_br_attn[i])
        x = x + merged @ w_o[i]
        h = rms_norm(x, ln2_g[i])
        x = x + conv_glu_ffn(h, w_ffn_up[i], conv_w[i], conv_b[i], w_ffn_down[i])
        h = rms_norm(x, ln3_g[i])
        x = x + (p[i] @ w_ple[i]) * jax.nn.sigmoid(h @ w_pg[i] + b_pg[i])
    return rms_norm(x, lnf_g)
```

```python
import functools
import math

import jax
import jax.numpy as jnp
from jax import lax
from jax.experimental import pallas as pl
from jax.experimental.pallas import tpu as pltpu

F32 = jnp.float32
BF16 = jnp.bfloat16
HIGHEST = lax.Precision.HIGHEST

LANES = 128
SUBLANES = 8
VMEM_LIMIT_BYTES = 56 * 1024 * 1024

HEAD_DIM = 64
PAIR = 2 * HEAD_DIM
CHUNK = 64
LEFT_CHUNKS = 8
BAND = (LEFT_CHUNKS + 1) * CHUNK
LEFT_PAD = LEFT_CHUNKS * CHUNK
MAX_REL = 128
DECAY_RANK = 64
ICLR_RANK = 64
GATE_RANK = 128
CONV_W = 3
NORM_EPS = 1e-6
GN_EPS = 64e-5
NEG_INF = -1e30
EXP_NEG_HALF = math.exp(-0.5)

assert PAIR == LANES


def _rms_norm(x, g):
    return x * lax.rsqrt(jnp.mean(x * x, axis=-1, keepdims=True) + NORM_EPS) * g


def _dot_hi(a, b, dims=None):
    if dims is None:
        return jnp.dot(a, b, precision=HIGHEST, preferred_element_type=F32)
    return lax.dot_general(a, b, (dims, ((), ())), precision=HIGHEST,
                           preferred_element_type=F32)


def _resident(shape):
    return pl.BlockSpec(shape, lambda *_: (0,) * len(shape), pipeline_mode=pl.Buffered(1))


def _inproj_kernel(x_ref, g_ref, w_ref, mu_ref, gb_ref, zs_ref, qkv_ref, gate_ref, carry_ref,
                   *, blocks_per_seq, rwkv_cols, attn_cols):
    i = pl.program_id(0)
    rows = x_ref.shape[0]
    hb = _rms_norm(x_ref[...], g_ref[...]).astype(BF16)

    z = jnp.dot(hb, w_ref[:, :rwkv_cols], preferred_element_type=F32)

    @pl.when(i % blocks_per_seq == 0)
    def _():
        carry_ref[...] = jnp.zeros_like(carry_ref)

    row = lax.broadcasted_iota(jnp.int32, z.shape, 0)
    prev = jnp.where(row == 0, carry_ref[0:1, :], pltpu.roll(z, 1, 0))
    zs_ref[...] = z + (prev - z) * mu_ref[...]
    carry_ref[0:1, :] = z[rows - 1:rows, :]

    qkv_ref[...] = jnp.dot(hb, w_ref[:, rwkv_cols:rwkv_cols + attn_cols],
                           preferred_element_type=F32).astype(BF16)
    zg = jnp.dot(hb, w_ref[:, rwkv_cols + attn_cols:], preferred_element_type=F32)
    gate_ref[...] = jax.nn.sigmoid(zg + gb_ref[...]).astype(BF16)


def _inproj(x2, g, w_bf, mu, gate_b, *, seq, rwkv_cols, attn_cols, tm):
    m, d = x2.shape
    gate_cols = w_bf.shape[1] - rwkv_cols - attn_cols
    body = functools.partial(_inproj_kernel, blocks_per_seq=seq // tm,
                             rwkv_cols=rwkv_cols, attn_cols=attn_cols)
    return pl.pallas_call(
        body,
        grid=(m // tm,),
        in_specs=[
            pl.BlockSpec((tm, d), lambda i: (i, 0)),
            _resident((1, d)),
            _resident(w_bf.shape),
            _resident((1, rwkv_cols)),
            _resident((1, gate_cols)),
        ],
        out_specs=[
            pl.BlockSpec((tm, rwkv_cols), lambda i: (i, 0)),
            pl.BlockSpec((tm, attn_cols), lambda i: (i, 0)),
            pl.BlockSpec((tm, gate_cols), lambda i: (i, 0)),
        ],
        out_shape=[
            jax.ShapeDtypeStruct((m, rwkv_cols), F32),
            jax.ShapeDtypeStruct((m, attn_cols), BF16),
            jax.ShapeDtypeStruct((m, gate_cols), BF16),
        ],
        scratch_shapes=[pltpu.VMEM((SUBLANES, rwkv_cols), F32)],
        compiler_params=pltpu.CompilerParams(
            dimension_semantics=("arbitrary",), vmem_limit_bytes=VMEM_LIMIT_BYTES),
        name="inproj",
    )(x2, g, w_bf, mu, gate_b)


def _unit_lower_inverse(a, eye, blk16, blk32):
    ad = jnp.where(blk16, a, 0.0)
    a2 = _dot_hi(ad, ad)
    a4 = _dot_hi(a2, a2)
    a8 = _dot_hi(a4, a4)
    t = _dot_hi(_dot_hi(eye - ad, eye + a2), _dot_hi(eye + a4, eye + a8))
    off1 = jnp.where(jnp.logical_and(blk32, jnp.logical_not(blk16)), a, 0.0)
    t = t - _dot_hi(t, _dot_hi(off1, t))
    off2 = jnp.where(blk32, 0.0, a)
    return t - _dot_hi(t, _dot_hi(off2, t))


def _rwkv_kernel(r_ref, k_ref, v_ref, wa_ref, gd_ref,
                 w0_ref, w2_ref, a0_ref, a2_ref, g2_ref, kk_ref, ka_ref, rk_ref, lnw_ref, lnb_ref,
                 o_ref,
                 state_ref, rs_ref, ks_ref, bs_ref, kps_ref, ld_ref, y_ref):
    tb = r_ref.shape[0]
    n_chunks = tb // CHUNK

    @pl.when(pl.program_id(2) == 0)
    def _():
        state_ref[...] = jnp.zeros_like(state_ref)

    lane = lax.broadcasted_iota(jnp.int32, (1, PAIR), 1)
    head0 = lane < HEAD_DIM
    rr = lax.broadcasted_iota(jnp.int32, (PAIR, PAIR), 0)
    cc = lax.broadcasted_iota(jnp.int32, (PAIR, PAIR), 1)
    same_head = (rr // HEAD_DIM) == (cc // HEAD_DIM)
    ones_blockdiag = same_head.astype(F32)

    def head_sum(t):
        return _dot_hi(t, ones_blockdiag)

    r = r_ref[...]
    k = k_ref[...]
    v = v_ref[...]
    wa = wa_ref[...]
    u = w0_ref[...] + _dot_hi(jnp.tanh(wa), w2_ref[...])
    ld_ref[...] = -EXP_NEG_HALF * jax.nn.sigmoid(u)
    a = jax.nn.sigmoid(a0_ref[...] + _dot_hi(wa, a2_ref[...]))
    g = _dot_hi(jax.nn.sigmoid(gd_ref[...]), g2_ref[...])
    kk = k * kk_ref[...]
    kk = kk / jnp.maximum(jnp.sqrt(head_sum(kk * kk)), 1e-12)
    kp = k * (1.0 + (a - 1.0) * ka_ref[...])
    rs_ref[...] = r
    ks_ref[...] = kk
    bs_ref[...] = kk * a
    kps_ref[...] = kp
    bonus = head_sum(r * kp * rk_ref[...]) * v

    ti = lax.broadcasted_iota(jnp.int32, (CHUNK, CHUNK), 0)
    si = lax.broadcasted_iota(jnp.int32, (CHUNK, CHUNK), 1)
    lower = si <= ti
    strict = si < ti
    blk16 = (ti // 16) == (si // 16)
    blk32 = (ti // 32) == (si // 32)
    eye = (ti == si).astype(F32)
    tri_ones = lower.astype(F32)

    def chunk_step(c, carry):
        rows = pl.ds(pl.multiple_of(c * CHUNK, CHUNK), CHUNK)
        ld = ld_ref[rows, :]
        cum = _dot_hi(tri_ones, ld)
        e_pos = jnp.exp(cum)
        e_neg = jnp.exp(-cum)
        rt = rs_ref[rows, :] * e_pos
        kt = ks_ref[rows, :] * jnp.exp(cum - ld)
        bt = bs_ref[rows, :] * e_neg
        kpt = kps_ref[rows, :] * e_neg
        vc = v_ref[rows, :]
        gamma_end = e_pos[CHUNK - 1:CHUNK, :]

        state = state_ref[...]
        from_state = _dot_hi(jnp.concatenate([kt, rt], axis=0), state, ((1,), (1,)))

        u_heads = []
        y_heads = []
        for hmask in (head0, jnp.logical_not(head0)):
            lhs = jnp.concatenate([jnp.where(hmask, kt, 0.0), jnp.where(hmask, rt, 0.0)], axis=0)
            sb = _dot_hi(lhs, bt, ((1,), (1,)))
            sk = _dot_hi(lhs, kpt, ((1,), (1,)))
            a_b = jnp.where(strict, sb[:CHUNK], 0.0)
            a_k = jnp.where(strict, sk[:CHUNK], 0.0)
            rk_rb = jnp.concatenate([jnp.where(lower, sk[CHUNK:], 0.0),
                                     jnp.where(lower, sb[CHUNK:], 0.0)], axis=1)
            w = from_state[:CHUNK] + _dot_hi(a_k, vc)
            t_inv = _unit_lower_inverse(a_b, eye, blk16, blk32)
            u_h = -_dot_hi(t_inv, w)
            u_heads.append(u_h)
            y_heads.append(_dot_hi(rk_rb, jnp.concatenate([vc, u_h], axis=0)))
        u_all = jnp.where(head0, u_heads[0], u_heads[1])
        y_ref[rows, :] = from_state[CHUNK:] + jnp.where(head0, y_heads[0], y_heads[1])

        upd = _dot_hi(jnp.concatenate([u_all, vc], axis=0),
                      jnp.concatenate([bt, kpt], axis=0), ((0,), (0,)))
        state_ref[...] = (state + jnp.where(same_head, upd, 0.0)) * gamma_end
        return carry

    lax.fori_loop(0, n_chunks, chunk_step, 0)

    y = y_ref[...]
    mean = head_sum(y) * (1.0 / HEAD_DIM)
    yc = y - mean
    var = head_sum(yc * yc) * (1.0 / HEAD_DIM)
    yn = yc * lax.rsqrt(var + GN_EPS) * lnw_ref[...] + lnb_ref[...]
    o_ref[...] = ((yn + bonus) * g).astype(o_ref.dtype)


def _rwkv(zs3, w0, w2p, a0, a2p, g2, k_k, k_a, r_k, lnx_w, lnx_b, *, width, tb):
    nb, seq, _ = zs3.shape
    n_pairs = width // PAIR
    lora_tile = 3 * n_pairs
    gate_tile = 3 * n_pairs + 1

    def tok(col_tile):
        return pl.BlockSpec((None, tb, PAIR), lambda b, p, t: (b, t, col_tile(p)))

    def par(rows):
        return pl.BlockSpec((rows, PAIR), lambda b, p, t: (0, p))

    block = pltpu.VMEM((tb, PAIR), F32)
    return pl.pallas_call(
        _rwkv_kernel,
        grid=(nb, n_pairs, seq // tb),
        in_specs=[
            tok(lambda p: p), tok(lambda p: n_pairs + p), tok(lambda p: 2 * n_pairs + p),
            tok(lambda p: lora_tile), tok(lambda p: gate_tile),
            par(1), par(PAIR), par(1), par(PAIR), par(GATE_RANK),
            par(1), par(1), par(1), par(1), par(1),
        ],
        out_specs=pl.BlockSpec((None, tb, PAIR), lambda b, p, t: (b, t, p)),
        out_shape=jax.ShapeDtypeStruct((nb, seq, width), BF16),
        scratch_shapes=[pltpu.VMEM((PAIR, PAIR), F32)] + [block] * 6,
        compiler_params=pltpu.CompilerParams(
            dimension_semantics=("parallel", "parallel", "arbitrary"),
            vmem_limit_bytes=VMEM_LIMIT_BYTES),
        name="rwkv",
    )(zs3, zs3, zs3, zs3, zs3, w0, w2p, a0, a2p, g2, k_k, k_a, r_k, lnx_w, lnx_b)


def _attn_kernel(q_ref, k_ref, v_ref, bias_ref, o_ref, kpad_ref, vpad_ref, *, scale):
    tq = q_ref.shape[0]
    seq = k_ref.shape[0]
    t = pl.program_id(2)

    @pl.when(t == 0)
    def _():
        zeros = jnp.zeros((LEFT_PAD, PAIR), kpad_ref.dtype)
        kpad_ref[0:LEFT_PAD, :] = zeros
        vpad_ref[0:LEFT_PAD, :] = zeros
        kpad_ref[LEFT_PAD:LEFT_PAD + seq, :] = k_ref[...]
        vpad_ref[LEFT_PAD:LEFT_PAD + seq, :] = v_ref[...]

    lane = lax.broadcasted_iota(jnp.int32, (1, PAIR), 1)
    head0 = lane < HEAD_DIM
    col = lax.broadcasted_iota(jnp.int32, (CHUNK, BAND), 1)

    def chunk_step(c, carry):
        rows = pl.ds(pl.multiple_of(c * CHUNK, CHUNK), CHUNK)
        chunk = t * (tq // CHUNK) + c
        band = pl.ds(pl.multiple_of(chunk * CHUNK, CHUNK), BAND)
        q = q_ref[rows, :]
        kb = kpad_ref[band, :]
        vb = vpad_ref[band, :]
        valid = col >= (LEFT_CHUNKS - chunk) * CHUNK
        outs = []
        for h, hmask in enumerate((head0, jnp.logical_not(head0))):
            qh = jnp.where(hmask, q, jnp.zeros_like(q))
            s = lax.dot_general(qh, kb, (((1,), (1,)), ((), ())), preferred_element_type=F32)
            s = jnp.where(valid, s * scale + bias_ref[h], NEG_INF)
            p = jnp.exp(s - jnp.max(s, axis=-1, keepdims=True))
            denom = jnp.sum(p, axis=-1, keepdims=True)
            outs.append(jnp.dot(p.astype(vb.dtype), vb, preferred_element_type=F32) / denom)
        o_ref[rows, :] = jnp.where(head0, outs[0], outs[1]).astype(o_ref.dtype)
        return carry

    lax.fori_loop(0, tq // CHUNK, chunk_step, 0)


def _attention(qkv3, bias, *, width, tq):
    nb, seq, _ = qkv3.shape
    n_pairs = width // PAIR
    body = functools.partial(_attn_kernel, scale=HEAD_DIM ** -0.5)
    padded = pltpu.VMEM((LEFT_PAD + seq, PAIR), BF16)
    return pl.pallas_call(
        body,
        grid=(nb, n_pairs, seq // tq),
        in_specs=[
            pl.BlockSpec((None, tq, PAIR), lambda b, p, t: (b, t, p)),
            pl.BlockSpec((None, seq, PAIR), lambda b, p, t: (b, 0, n_pairs + p)),
            pl.BlockSpec((None, seq, PAIR), lambda b, p, t: (b, 0, 2 * n_pairs + p)),
            pl.BlockSpec((2, CHUNK, BAND), lambda b, p, t: (p, 0, 0)),
        ],
        out_specs=pl.BlockSpec((None, tq, PAIR), lambda b, p, t: (b, t, p)),
        out_shape=jax.ShapeDtypeStruct((nb, seq, width), BF16),
        scratch_shapes=[padded, padded],
        compiler_params=pltpu.CompilerParams(
            dimension_semantics=("parallel", "parallel", "arbitrary"),
            vmem_limit_bytes=VMEM_LIMIT_BYTES),
        name="chunk_attention",
    )(qkv3, qkv3, qkv3, bias)


def _band_bias(rel_bias):
    rel = jnp.arange(CHUNK)[:, None] - (jnp.arange(BAND) - LEFT_PAD)[None, :]
    idx = jnp.clip(rel, -(CHUNK - 1), MAX_REL) + (CHUNK - 1)
    return rel_bias.astype(F32)[:, idx]


def _merge_kernel(x_ref, oa_ref, ob_ref, gate_ref, wa_ref, wb_ref, wo_ref, o_ref):
    d = x_ref.shape[1]
    br_a = jnp.dot(oa_ref[...], wa_ref[...], preferred_element_type=F32)
    br_b = jnp.dot(ob_ref[...], wb_ref[...], preferred_element_type=F32)
    merged = gate_ref[:, :d].astype(F32) * br_a + gate_ref[:, d:].astype(F32) * br_b
    o_ref[...] = x_ref[...] + jnp.dot(merged.astype(BF16), wo_ref[...], preferred_element_type=F32)


def _merge(x2, oa, ob, gates, w_a, w_b, w_o, *, tm):
    m, d = x2.shape
    row = lambda cols: pl.BlockSpec((tm, cols), lambda i: (i, 0))
    return pl.pallas_call(
        _merge_kernel,
        grid=(m // tm,),
        in_specs=[row(d), row(oa.shape[1]), row(ob.shape[1]), row(gates.shape[1]),
                  _resident(w_a.shape), _resident(w_b.shape), _resident(w_o.shape)],
        out_specs=row(d),
        out_shape=jax.ShapeDtypeStruct((m, d), F32),
        compiler_params=pltpu.CompilerParams(
            dimension_semantics=("parallel",), vmem_limit_bytes=VMEM_LIMIT_BYTES),
        name="merge",
    )(x2, oa, ob, gates, w_a, w_b, w_o)


def _ffn_kernel(x_ref, p_ref, g2_ref, wup_ref, cw_ref, cb_ref, wdn_ref, g3_ref, wple_ref, wpg_ref,
                bpg_ref, gf_ref, o_ref, carry_ref, *, blocks_per_seq, d_ff, ff_tile, final_norm):
    i = pl.program_id(0)
    rows = x_ref.shape[0]
    x = x_ref[...]
    hb = _rms_norm(x, g2_ref[...]).astype(BF16)

    @pl.when(i % blocks_per_seq == 0)
    def _():
        carry_ref[...] = jnp.zeros_like(carry_ref)

    row = lax.broadcasted_iota(jnp.int32, (rows, ff_tile), 0)
    acc = x
    for j in range(d_ff // ff_tile):
        cols = slice(j * ff_tile, (j + 1) * ff_tile)
        a = jnp.dot(hb, wup_ref[:, cols], preferred_element_type=F32)
        gv = jnp.dot(hb, wup_ref[:, d_ff + j * ff_tile:d_ff + (j + 1) * ff_tile],
                     preferred_element_type=F32)
        c0 = carry_ref[0:1, cols]
        c1 = carry_ref[1:2, cols]
        prev1 = jnp.where(row == 0, c1, pltpu.roll(a, 1, 0))
        prev2 = jnp.where(row == 0, c0, jnp.where(row == 1, c1, pltpu.roll(a, 2, 0)))
        carry_ref[0:2, cols] = a[rows - 2:rows, :]
        conv = (cw_ref[0:1, cols] * prev2 + cw_ref[1:2, cols] * prev1 + cw_ref[2:3, cols] * a
                + cb_ref[:, cols])
        act = 0.5 * conv * (1.0 + lax.erf(conv * math.sqrt(0.5))) * gv
        acc = acc + jnp.dot(act.astype(BF16), wdn_ref[cols, :], preferred_element_type=F32)

    h3 = _rms_norm(acc, g3_ref[...]).astype(BF16)
    pg = jax.nn.sigmoid(jnp.dot(h3, wpg_ref[...], preferred_element_type=F32) + bpg_ref[...])
    ple = jnp.dot(p_ref[...].astype(BF16), wple_ref[...], preferred_element_type=F32)
    out = acc + ple * pg
    if final_norm:
        out = _rms_norm(out, gf_ref[...])
    o_ref[...] = out


def _ffn(x2, p2, g2, w_up, conv_w, conv_b, w_down, g3, w_ple, w_pg, b_pg, gf, *, seq, tm,
         final_norm):
    m, d = x2.shape
    d_ff = w_down.shape[0]
    ff_tile = d_ff // 2 if (d_ff // 2) % LANES == 0 else d_ff
    body = functools.partial(_ffn_kernel, blocks_per_seq=seq // tm, d_ff=d_ff, ff_tile=ff_tile,
                             final_norm=final_norm)
    row = lambda cols: pl.BlockSpec((tm, cols), lambda i: (i, 0))
    return pl.pallas_call(
        body,
        grid=(m // tm,),
        in_specs=[row(d), row(p2.shape[1]), _resident((1, d)), _resident(w_up.shape),
                  _resident(conv_w.shape), _resident((1, d_ff)), _resident(w_down.shape),
                  _resident((1, d)), _resident(w_ple.shape), _resident(w_pg.shape),
                  _resident((1, d)), _resident((1, d))],
        out_specs=row(d),
        out_shape=jax.ShapeDtypeStruct((m, d), F32),
        scratch_shapes=[pltpu.VMEM((SUBLANES, d_ff), F32)],
        compiler_params=pltpu.CompilerParams(
            dimension_semantics=("arbitrary",), vmem_limit_bytes=VMEM_LIMIT_BYTES),
        name="ffn",
    )(x2, p2, g2, w_up, conv_w, conv_b, w_down, g3, w_ple, w_pg, b_pg, gf)


def kernel(x, p, ln1_g, w_in, mix_mu, w0, w2, a0, a2, g2, k_k, k_a, r_k, lnx_w, lnx_b, rel_bias, gate_b, w_br_rwkv, w_br_attn, w_o, ln2_g, w_ffn_up, conv_w, conv_b, w_ffn_down, ln3_g, w_ple, w_pg, b_pg, lnf_g):
    nb, seq, d = x.shape
    depth = w_in.shape[0]
    rwkv_width = w_br_rwkv.shape[1]
    attn_width = w_br_attn.shape[1]
    rwkv_cols = 3 * rwkv_width + DECAY_RANK + ICLR_RANK + GATE_RANK
    attn_cols = 3 * attn_width
    assert DECAY_RANK + ICLR_RANK == LANES and GATE_RANK == LANES
    assert seq % 512 == 0 and seq >= BAND
    row = lambda t: t.reshape(1, -1)
    m = nb * seq
    x2 = x.reshape(m, d)

    for i in range(depth):
        last = i == depth - 1
        zs, qkv, gates = _inproj(
            x2, row(ln1_g[i]), w_in[i].astype(BF16), row(mix_mu[i]), row(gate_b[i]),
            seq=seq, rwkv_cols=rwkv_cols, attn_cols=attn_cols, tm=256)

        w2p = jnp.concatenate([w2[i], jnp.zeros_like(a2[i])], axis=0)
        a2p = jnp.concatenate([jnp.zeros_like(w2[i]), a2[i]], axis=0)
        o_a = _rwkv(zs.reshape(nb, seq, rwkv_cols), row(w0[i]), w2p, row(a0[i]), a2p, g2[i],
                    row(k_k[i]), row(k_a[i]), row(r_k[i]), row(lnx_w[i]), row(lnx_b[i]),
                    width=rwkv_width, tb=512)

        o_b = _attention(qkv.reshape(nb, seq, attn_cols), _band_bias(rel_bias[i]),
                         width=attn_width, tq=512)

        x2 = _merge(x2, o_a.reshape(m, rwkv_width), o_b.reshape(m, attn_width), gates,
                    w_br_rwkv[i].astype(BF16), w_br_attn[i].astype(BF16), w_o[i].astype(BF16),
                    tm=512)

        x2 = _ffn(x2, p[i].reshape(m, -1), row(ln2_g[i]), w_ffn_up[i].astype(BF16), conv_w[i],
                  row(conv_b[i]), w_ffn_down[i].astype(BF16), row(ln3_g[i]),
                  w_ple[i].astype(BF16), w_pg[i].astype(BF16), row(b_pg[i]), row(lnf_g),
                  seq=seq, tm=256, final_norm=last)
    return x2.reshape(nb, seq, d)
```

```python
import functools
import math

import jax
import jax.numpy as jnp
from jax import lax
from jax.experimental import pallas as pl
from jax.experimental.pallas import tpu as pltpu

F32 = jnp.float32
BF16 = jnp.bfloat16

LANES = 128
SUBLANES = 8
VMEM_LIMIT_BYTES = 56 * 1024 * 1024

HEAD_DIM = 64
PAIR = 2 * HEAD_DIM
CHUNK = 64
LEFT_CHUNKS = 8
BAND = (LEFT_CHUNKS + 1) * CHUNK
LEFT_PAD = LEFT_CHUNKS * CHUNK
MAX_REL = 128
DECAY_RANK = 64
ICLR_RANK = 64
GATE_RANK = 128
CONV_W = 3
NORM_EPS = 1e-6
GN_EPS = 64e-5
NEG_INF = -1e30
EXP_NEG_HALF = math.exp(-0.5)
ATTN_GROUP = 4

assert PAIR == LANES


def _rms_norm(x, g):
    return x * lax.rsqrt(jnp.mean(x * x, axis=-1, keepdims=True) + NORM_EPS) * g


def _resident(shape):
    return pl.BlockSpec(shape, lambda *_: (0,) * len(shape), pipeline_mode=pl.Buffered(1))


def _inproj_kernel(x_ref, g_ref, w_ref, mu_ref, gb_ref, zs_ref, qkv_ref, gate_ref, carry_ref,
                   *, blocks_per_seq, rwkv_cols, attn_cols):
    i = pl.program_id(0)
    rows = x_ref.shape[0]
    hb = _rms_norm(x_ref[...], g_ref[...]).astype(BF16)

    z = jnp.dot(hb, w_ref[:, :rwkv_cols], preferred_element_type=F32)

    @pl.when(i % blocks_per_seq == 0)
    def _():
        carry_ref[...] = jnp.zeros_like(carry_ref)

    row = lax.broadcasted_iota(jnp.int32, z.shape, 0)
    prev = jnp.where(row == 0, carry_ref[0:1, :], pltpu.roll(z, 1, 0))
    zs_ref[...] = z + (prev - z) * mu_ref[...]
    carry_ref[0:1, :] = z[rows - 1:rows, :]

    qkv_ref[...] = jnp.dot(hb, w_ref[:, rwkv_cols:rwkv_cols + attn_cols],
                           preferred_element_type=F32).astype(BF16)
    zg = jnp.dot(hb, w_ref[:, rwkv_cols + attn_cols:], preferred_element_type=F32)
    gate_ref[...] = jax.nn.sigmoid(zg + gb_ref[...]).astype(BF16)


def _inproj(x2, g, w_bf, mu, gate_b, *, seq, rwkv_cols, attn_cols, tm):
    m, d = x2.shape
    gate_cols = w_bf.shape[1] - rwkv_cols - attn_cols
    body = functools.partial(_inproj_kernel, blocks_per_seq=seq // tm,
                             rwkv_cols=rwkv_cols, attn_cols=attn_cols)
    return pl.pallas_call(
        body,
        grid=(m // tm,),
        in_specs=[
            pl.BlockSpec((tm, d), lambda i: (i, 0)),
            _resident((1, d)),
            _resident(w_bf.shape),
            _resident((1, rwkv_cols)),
            _resident((1, gate_cols)),
        ],
        out_specs=[
            pl.BlockSpec((tm, rwkv_cols), lambda i: (i, 0)),
            pl.BlockSpec((tm, attn_cols), lambda i: (i, 0)),
            pl.BlockSpec((tm, gate_cols), lambda i: (i, 0)),
        ],
        out_shape=[
            jax.ShapeDtypeStruct((m, rwkv_cols), F32),
            jax.ShapeDtypeStruct((m, attn_cols), BF16),
            jax.ShapeDtypeStruct((m, gate_cols), BF16),
        ],
        scratch_shapes=[pltpu.VMEM((SUBLANES, rwkv_cols), F32)],
        compiler_params=pltpu.CompilerParams(
            dimension_semantics=("arbitrary",), vmem_limit_bytes=VMEM_LIMIT_BYTES),
        name="inproj",
    )(x2, g, w_bf, mu, gate_b)


def _mm(a, b, dims=None):
    a = a.astype(BF16)
    b = b.astype(BF16)
    if dims is None:
        return jnp.dot(a, b, preferred_element_type=F32)
    return lax.dot_general(a, b, (dims, ((), ())), preferred_element_type=F32)


_NT = ((1,), (1,))
_TN = ((0,), (0,))


def _split2(x):
    hi = x.astype(BF16)
    return hi, (x - hi.astype(F32)).astype(BF16)


def _split3(x):
    hi = x.astype(BF16)
    r1 = x - hi.astype(F32)
    mid = r1.astype(BF16)
    return hi, mid, (r1 - mid.astype(F32)).astype(BF16)


def _unit_lower_inverse_minus_eye(mats, blk16, blk32):
    ad = [jnp.where(blk16, a, 0.0) for a in mats]
    a2 = [_mm(a, a) for a in ad]
    a4 = [_mm(a, a) for a in a2]
    p1 = [_mm(a, b) for a, b in zip(ad, a2)]
    a8 = [_mm(a, a) for a in a4]
    p2 = [_mm(a, b) for a, b in zip(a4, a8)]
    x1 = [b - a - p for a, b, p in zip(ad, a2, p1)]
    x2 = [a + b + p for a, b, p in zip(a4, a8, p2)]
    p3 = [_mm(a, b) for a, b in zip(x1, x2)]
    m = [a + b + p for a, b, p in zip(x1, x2, p3)]
    for mask in (jnp.logical_and(blk32, jnp.logical_not(blk16)), jnp.logical_not(blk32)):
        off = [jnp.where(mask, a, 0.0) for a in mats]
        z = [o + _mm(mi, o) for mi, o in zip(m, off)]
        m = [mi - zi - _mm(zi, mi) for mi, zi in zip(m, z)]
    return m


def _rwkv_kernel(r_ref, k_ref, v_ref, wa_ref, gd_ref,
                 w0_ref, w2_ref, a0_ref, a2_ref, g2_ref, kk_ref, ka_ref, rk_ref, lnw_ref, lnb_ref,
                 o_ref, state_ref):
    tb = r_ref.shape[0]
    n_chunks = tb // CHUNK

    @pl.when(pl.program_id(2) == 0)
    def _():
        state_ref[...] = jnp.zeros_like(state_ref)

    lane = lax.broadcasted_iota(jnp.int32, (1, PAIR), 1)
    head0 = lane < HEAD_DIM
    head_masks = (head0, jnp.logical_not(head0))
    rr = lax.broadcasted_iota(jnp.int32, (PAIR, PAIR), 0)
    cc = lax.broadcasted_iota(jnp.int32, (PAIR, PAIR), 1)
    same_head = (rr // HEAD_DIM) == (cc // HEAD_DIM)
    ones2 = jnp.concatenate([same_head.astype(BF16)] * 2, axis=0)

    def head_sum(t):
        return jnp.dot(jnp.concatenate(_split2(t), axis=1), ones2, preferred_element_type=F32)

    r = r_ref[...]
    k = k_ref[...]
    v = v_ref[...]
    wa = wa_ref[...]
    u = w0_ref[...] + _mm(jnp.tanh(wa), w2_ref[...])
    ld = -EXP_NEG_HALF * jax.nn.sigmoid(u)
    a = jax.nn.sigmoid(a0_ref[...] + _mm(wa, a2_ref[...]))
    g = _mm(jax.nn.sigmoid(gd_ref[...]), g2_ref[...])
    kk = k * kk_ref[...]
    kk = kk / jnp.maximum(jnp.sqrt(head_sum(kk * kk)), 1e-12)
    kp = k * (1.0 + (a - 1.0) * ka_ref[...])
    bonus = head_sum(r * kp * rk_ref[...]) * v

    ti = lax.broadcasted_iota(jnp.int32, (CHUNK, CHUNK), 0)
    si = lax.broadcasted_iota(jnp.int32, (CHUNK, CHUNK), 1)
    lower = si <= ti
    strict = si < ti
    blk16 = (ti // 16) == (si // 16)
    blk32 = (ti // 32) == (si // 32)
    tri3 = jnp.concatenate([lower.astype(BF16)] * 3, axis=1)

    ld_parts = _split3(ld)
    cum = jnp.concatenate([
        jnp.dot(tri3, jnp.concatenate([part[c * CHUNK:(c + 1) * CHUNK] for part in ld_parts], axis=0),
                preferred_element_type=F32)
        for c in range(n_chunks)], axis=0)
    e_pos = jnp.exp(cum)
    e_neg = jnp.exp(-cum)
    rt_all = r * e_pos
    kt_all = kk * jnp.exp(cum - ld)
    bt_all = kk * a * e_neg
    kpt_all = kp * e_neg

    chunks = range(n_chunks)
    rows = [slice(c * CHUNK, (c + 1) * CHUNK) for c in chunks]
    rt = [rt_all[s] for s in rows]
    kt = [kt_all[s] for s in rows]
    bt = [bt_all[s] for s in rows]
    kpt = [kpt_all[s] for s in rows]
    vc = [v[s] for s in rows]
    gamma_end = [e_pos[s][CHUNK - 1:CHUNK, :] for s in rows]
    zeros = jnp.zeros((CHUNK, PAIR), F32)
    inst = [(c, hmask) for c in chunks for hmask in head_masks]

    kt_h = [jnp.where(hmask, kt[c], 0.0) for c, hmask in inst]
    lhs = [jnp.concatenate([kh, jnp.where(hmask, rt[c], 0.0)], axis=0)
           for kh, (c, hmask) in zip(kt_h, inst)]
    sb = [_mm(l, bt[c], _NT) for l, (c, _) in zip(lhs, inst)]
    sk = [_mm(l, kpt[c], _NT) for l, (c, _) in zip(lhs, inst)]
    a_b = [jnp.where(strict, s[:CHUNK], 0.0) for s in sb]
    a_k = [jnp.where(strict, s[:CHUNK], 0.0) for s in sk]
    rk_rb = [jnp.concatenate([jnp.where(lower, s_k[CHUNK:], 0.0),
                              jnp.where(lower, s_b[CHUNK:], 0.0)], axis=1)
             for s_k, s_b in zip(sk, sb)]
    akv = [_mm(ak, vc[c]) for ak, (c, _) in zip(a_k, inst)]
    t_m = _unit_lower_inverse_minus_eye(a_b, blk16, blk32)
    x = [jnp.concatenate([av, kh], axis=1) for av, kh in zip(akv, kt_h)]
    fe = [xi + _mm(tm, xi) for tm, xi in zip(t_m, x)]
    yy = [_mm(rr, jnp.concatenate([jnp.concatenate([vc[c], zeros], axis=1), -f_e], axis=0))
          for rr, f_e, (c, _) in zip(rk_rb, fe, inst)]

    e_sum = [fe[2 * c][:, PAIR:] + fe[2 * c + 1][:, PAIR:] for c in chunks]
    f_all = [jnp.where(head0, fe[2 * c][:, :PAIR], fe[2 * c + 1][:, :PAIR]) for c in chunks]
    y_own = [jnp.where(head0, yy[2 * c][:, :PAIR], yy[2 * c + 1][:, :PAIR]) for c in chunks]
    ry = [rt[c] + yy[2 * c][:, PAIR:] + yy[2 * c + 1][:, PAIR:] for c in chunks]
    b_e = [jnp.where(same_head, _mm(bt[c], e_sum[c], _TN), 0.0) for c in chunks]
    q_g = [jnp.where(same_head, _mm(jnp.concatenate([vc[c], -f_all[c]], axis=0),
                                    jnp.concatenate([kpt[c], bt[c]], axis=0), _TN), 0.0)
           for c in chunks]

    state = state_ref[...]
    y_chunks = []
    for c in chunks:
        y_chunks.append(_mm(ry[c], state, _NT) + y_own[c])
        state = (state - _mm(state, b_e[c], _NT) + q_g[c]) * gamma_end[c]
    state_ref[...] = state

    y = jnp.concatenate(y_chunks, axis=0)
    mean = head_sum(y) * (1.0 / HEAD_DIM)
    yc = y - mean
    var = head_sum(yc * yc) * (1.0 / HEAD_DIM)
    yn = yc * lax.rsqrt(var + GN_EPS) * lnw_ref[...] + lnb_ref[...]
    o_ref[...] = ((yn + bonus) * g).astype(o_ref.dtype)


def _rwkv(zs3, w0, w2p, a0, a2p, g2, k_k, k_a, r_k, lnx_w, lnx_b, *, width, tb):
    nb, seq, _ = zs3.shape
    n_pairs = width // PAIR
    lora_tile = 3 * n_pairs
    gate_tile = 3 * n_pairs + 1

    def tok(col_tile):
        return pl.BlockSpec((None, tb, PAIR), lambda b, p, t: (b, t, col_tile(p)))

    def par(rows):
        return pl.BlockSpec((rows, PAIR), lambda b, p, t: (0, p))

    return pl.pallas_call(
        _rwkv_kernel,
        grid=(nb, n_pairs, seq // tb),
        in_specs=[
            tok(lambda p: p), tok(lambda p: n_pairs + p), tok(lambda p: 2 * n_pairs + p),
            tok(lambda p: lora_tile), tok(lambda p: gate_tile),
            par(1), par(PAIR), par(1), par(PAIR), par(GATE_RANK),
            par(1), par(1), par(1), par(1), par(1),
        ],
        out_specs=pl.BlockSpec((None, tb, PAIR), lambda b, p, t: (b, t, p)),
        out_shape=jax.ShapeDtypeStruct((nb, seq, width), BF16),
        scratch_shapes=[pltpu.VMEM((PAIR, PAIR), F32)],
        compiler_params=pltpu.CompilerParams(
            dimension_semantics=("parallel", "parallel", "arbitrary"),
            vmem_limit_bytes=VMEM_LIMIT_BYTES),
        name="rwkv",
    )(zs3, zs3, zs3, zs3, zs3, w0, w2p, a0, a2p, g2, k_k, k_a, r_k, lnx_w, lnx_b)


def _attn_kernel(q_ref, k_ref, v_ref, bias_ref, o_ref, kpad_ref, vpad_ref, *, scale, group):
    tq = q_ref.shape[0]
    seq = k_ref.shape[0]
    t = pl.program_id(2)

    @pl.when(t == 0)
    def _():
        zeros = jnp.zeros((LEFT_PAD, PAIR), kpad_ref.dtype)
        kpad_ref[0:LEFT_PAD, :] = zeros
        vpad_ref[0:LEFT_PAD, :] = zeros
        kpad_ref[LEFT_PAD:LEFT_PAD + seq, :] = k_ref[...]
        vpad_ref[LEFT_PAD:LEFT_PAD + seq, :] = v_ref[...]

    lane = lax.broadcasted_iota(jnp.int32, (1, PAIR), 1)
    head0 = lane < HEAD_DIM
    col = lax.broadcasted_iota(jnp.int32, (CHUNK, BAND), 1)

    head_masks = (head0, jnp.logical_not(head0))

    def group_step(g, carry):
        first = t * (tq // CHUNK) + g * group
        q, kb, vb, valid = [], [], [], []
        for j in range(group):
            rows = pl.ds(pl.multiple_of((g * group + j) * CHUNK, CHUNK), CHUNK)
            band = pl.ds(pl.multiple_of((first + j) * CHUNK, CHUNK), BAND)
            q.append(q_ref[rows, :])
            kb.append(kpad_ref[band, :])
            vb.append(vpad_ref[band, :])
            valid.append(col >= (LEFT_CHUNKS - first - j) * CHUNK)
        inst = [(j, h) for j in range(group) for h in range(2)]
        s = [lax.dot_general(jnp.where(head_masks[h], q[j], jnp.zeros_like(q[j])), kb[j],
                             (((1,), (1,)), ((), ())), preferred_element_type=F32)
             for j, h in inst]
        s = [jnp.where(valid[j], si * scale + bias_ref[h], NEG_INF) for si, (j, h) in zip(s, inst)]
        p = [jnp.exp(si - jnp.max(si, axis=-1, keepdims=True)) for si in s]
        denom = [jnp.sum(pi, axis=-1, keepdims=True) for pi in p]
        o = [jnp.dot(pi.astype(vb[j].dtype), vb[j], preferred_element_type=F32)
             for pi, (j, _) in zip(p, inst)]
        o = [oi / di for oi, di in zip(o, denom)]
        for j in range(group):
            rows = pl.ds(pl.multiple_of((g * group + j) * CHUNK, CHUNK), CHUNK)
            o_ref[rows, :] = jnp.where(head0, o[2 * j], o[2 * j + 1]).astype(o_ref.dtype)
        return carry

    lax.fori_loop(0, tq // (CHUNK * group), group_step, 0)


def _attention(qkv3, bias, *, width, tq):
    nb, seq, _ = qkv3.shape
    n_pairs = width // PAIR
    body = functools.partial(_attn_kernel, scale=HEAD_DIM ** -0.5, group=ATTN_GROUP)
    padded = pltpu.VMEM((LEFT_PAD + seq, PAIR), BF16)
    return pl.pallas_call(
        body,
        grid=(nb, n_pairs, seq // tq),
        in_specs=[
            pl.BlockSpec((None, tq, PAIR), lambda b, p, t: (b, t, p)),
            pl.BlockSpec((None, seq, PAIR), lambda b, p, t: (b, 0, n_pairs + p)),
            pl.BlockSpec((None, seq, PAIR), lambda b, p, t: (b, 0, 2 * n_pairs + p)),
            pl.BlockSpec((2, CHUNK, BAND), lambda b, p, t: (p, 0, 0)),
        ],
        out_specs=pl.BlockSpec((None, tq, PAIR), lambda b, p, t: (b, t, p)),
        out_shape=jax.ShapeDtypeStruct((nb, seq, width), BF16),
        scratch_shapes=[padded, padded],
        compiler_params=pltpu.CompilerParams(
            dimension_semantics=("parallel", "parallel", "arbitrary"),
            vmem_limit_bytes=VMEM_LIMIT_BYTES),
        name="chunk_attention",
    )(qkv3, qkv3, qkv3, bias)


def _band_bias(rel_bias):
    rel = jnp.arange(CHUNK)[:, None] - (jnp.arange(BAND) - LEFT_PAD)[None, :]
    idx = jnp.clip(rel, -(CHUNK - 1), MAX_REL) + (CHUNK - 1)
    return rel_bias.astype(F32)[:, idx]


def _merge_kernel(x_ref, oa_ref, ob_ref, gate_ref, wa_ref, wb_ref, wo_ref, o_ref):
    d = x_ref.shape[1]
    br_a = jnp.dot(oa_ref[...], wa_ref[...], preferred_element_type=F32)
    br_b = jnp.dot(ob_ref[...], wb_ref[...], preferred_element_type=F32)
    merged = gate_ref[:, :d].astype(F32) * br_a + gate_ref[:, d:].astype(F32) * br_b
    o_ref[...] = x_ref[...] + jnp.dot(merged.astype(BF16), wo_ref[...], preferred_element_type=F32)


def _merge(x2, oa, ob, gates, w_a, w_b, w_o, *, tm):
    m, d = x2.shape
    row = lambda cols: pl.BlockSpec((tm, cols), lambda i: (i, 0))
    return pl.pallas_call(
        _merge_kernel,
        grid=(m // tm,),
        in_specs=[row(d), row(oa.shape[1]), row(ob.shape[1]), row(gates.shape[1]),
                  _resident(w_a.shape), _resident(w_b.shape), _resident(w_o.shape)],
        out_specs=row(d),
        out_shape=jax.ShapeDtypeStruct((m, d), F32),
        compiler_params=pltpu.CompilerParams(
            dimension_semantics=("parallel",), vmem_limit_bytes=VMEM_LIMIT_BYTES),
        name="merge",
    )(x2, oa, ob, gates, w_a, w_b, w_o)


def _ffn_kernel(x_ref, p_ref, g2_ref, wup_ref, cw_ref, cb_ref, wdn_ref, g3_ref, wple_ref, wpg_ref,
                bpg_ref, gf_ref, o_ref, carry_ref, *, blocks_per_seq, d_ff, ff_tile, final_norm):
    i = pl.program_id(0)
    rows = x_ref.shape[0]
    x = x_ref[...]
    hb = _rms_norm(x, g2_ref[...]).astype(BF16)

    @pl.when(i % blocks_per_seq == 0)
    def _():
        carry_ref[...] = jnp.zeros_like(carry_ref)

    row = lax.broadcasted_iota(jnp.int32, (rows, ff_tile), 0)
    acc = x
    for j in range(d_ff // ff_tile):
        cols = slice(j * ff_tile, (j + 1) * ff_tile)
        a = jnp.dot(hb, wup_ref[:, cols], preferred_element_type=F32)
        gv = jnp.dot(hb, wup_ref[:, d_ff + j * ff_tile:d_ff + (j + 1) * ff_tile],
                     preferred_element_type=F32)
        c0 = carry_ref[0:1, cols]
        c1 = carry_ref[1:2, cols]
        prev1 = jnp.where(row == 0, c1, pltpu.roll(a, 1, 0))
        prev2 = jnp.where(row == 0, c0, jnp.where(row == 1, c1, pltpu.roll(a, 2, 0)))
        carry_ref[0:2, cols] = a[rows - 2:rows, :]
        conv = (cw_ref[0:1, cols] * prev2 + cw_ref[1:2, cols] * prev1 + cw_ref[2:3, cols] * a
                + cb_ref[:, cols])
        act = 0.5 * conv * (1.0 + lax.erf(conv * math.sqrt(0.5))) * gv
        acc = acc + jnp.dot(act.astype(BF16), wdn_ref[cols, :], preferred_element_type=F32)

    h3 = _rms_norm(acc, g3_ref[...]).astype(BF16)
    pg = jax.nn.sigmoid(jnp.dot(h3, wpg_ref[...], preferred_element_type=F32) + bpg_ref[...])
    ple = jnp.dot(p_ref[...].astype(BF16), wple_ref[...], preferred_element_type=F32)
    out = acc + ple * pg
    if final_norm:
        out = _rms_norm(out, gf_ref[...])
    o_ref[...] = out


def _ffn(x2, p2, g2, w_up, conv_w, conv_b, w_down, g3, w_ple, w_pg, b_pg, gf, *, seq, tm,
         final_norm):
    m, d = x2.shape
    d_ff = w_down.shape[0]
    ff_tile = d_ff // 2 if (d_ff // 2) % LANES == 0 else d_ff
    body = functools.partial(_ffn_kernel, blocks_per_seq=seq // tm, d_ff=d_ff, ff_tile=ff_tile,
                             final_norm=final_norm)
    row = lambda cols: pl.BlockSpec((tm, cols), lambda i: (i, 0))
    return pl.pallas_call(
        body,
        grid=(m // tm,),
        in_specs=[row(d), row(p2.shape[1]), _resident((1, d)), _resident(w_up.shape),
                  _resident(conv_w.shape), _resident((1, d_ff)), _resident(w_down.shape),
                  _resident((1, d)), _resident(w_ple.shape), _resident(w_pg.shape),
                  _resident((1, d)), _resident((1, d))],
        out_specs=row(d),
        out_shape=jax.ShapeDtypeStruct((m, d), F32),
        scratch_shapes=[pltpu.VMEM((SUBLANES, d_ff), F32)],
        compiler_params=pltpu.CompilerParams(
            dimension_semantics=("arbitrary",), vmem_limit_bytes=VMEM_LIMIT_BYTES),
        name="ffn",
    )(x2, p2, g2, w_up, conv_w, conv_b, w_down, g3, w_ple, w_pg, b_pg, gf)


def kernel(x, p, ln1_g, w_in, mix_mu, w0, w2, a0, a2, g2, k_k, k_a, r_k, lnx_w, lnx_b, rel_bias, gate_b, w_br_rwkv, w_br_attn, w_o, ln2_g, w_ffn_up, conv_w, conv_b, w_ffn_down, ln3_g, w_ple, w_pg, b_pg, lnf_g):
    nb, seq, d = x.shape
    depth = w_in.shape[0]
    rwkv_width = w_br_rwkv.shape[1]
    attn_width = w_br_attn.shape[1]
    rwkv_cols = 3 * rwkv_width + DECAY_RANK + ICLR_RANK + GATE_RANK
    attn_cols = 3 * attn_width
    assert DECAY_RANK + ICLR_RANK == LANES and GATE_RANK == LANES
    assert seq % 512 == 0 and seq >= BAND
    row = lambda t: t.reshape(1, -1)
    m = nb * seq
    x2 = x.reshape(m, d)

    for i in range(depth):
        last = i == depth - 1
        zs, qkv, gates = _inproj(
            x2, row(ln1_g[i]), w_in[i].astype(BF16), row(mix_mu[i]), row(gate_b[i]),
            seq=seq, rwkv_cols=rwkv_cols, attn_cols=attn_cols, tm=256)

        w2p = jnp.concatenate([w2[i], jnp.zeros_like(a2[i])], axis=0)
        a2p = jnp.concatenate([jnp.zeros_like(w2[i]), a2[i]], axis=0)
        o_a = _rwkv(zs.reshape(nb, seq, rwkv_cols), row(w0[i]), w2p, row(a0[i]), a2p, g2[i],
                    row(k_k[i]), row(k_a[i]), row(r_k[i]), row(lnx_w[i]), row(lnx_b[i]),
                    width=rwkv_width, tb=512)

        o_b = _attention(qkv.reshape(nb, seq, attn_cols), _band_bias(rel_bias[i]),
                         width=attn_width, tq=512)

        x2 = _merge(x2, o_a.reshape(m, rwkv_width), o_b.reshape(m, attn_width), gates,
                    w_br_rwkv[i].astype(BF16), w_br_attn[i].astype(BF16), w_o[i].astype(BF16),
                    tm=512)

        x2 = _ffn(x2, p[i].reshape(m, -1), row(ln2_g[i]), w_ffn_up[i].astype(BF16), conv_w[i],
                  row(conv_b[i]), w_ffn_down[i].astype(BF16), row(ln3_g[i]),
                  w_ple[i].astype(BF16), w_pg[i].astype(BF16), row(b_pg[i]), row(lnf_g),
                  seq=seq, tm=256, final_norm=last)
    return x2.reshape(nb, seq, d)
```

```python
import functools
import math

import jax
import jax.numpy as jnp
from jax import lax
from jax.experimental import pallas as pl
from jax.experimental.pallas import tpu as pltpu

F32 = jnp.float32
BF16 = jnp.bfloat16

LANES = 128
SUBLANES = 8
VMEM_LIMIT_BYTES = 56 * 1024 * 1024

HEAD_DIM = 64
PAIR = 2 * HEAD_DIM
CHUNK = 64
LEFT_CHUNKS = 8
BAND = (LEFT_CHUNKS + 1) * CHUNK
LEFT_PAD = LEFT_CHUNKS * CHUNK
MAX_REL = 128
DECAY_RANK = 64
ICLR_RANK = 64
GATE_RANK = 128
CONV_W = 3
NORM_EPS = 1e-6
GN_EPS = 64e-5
NEG_INF = -1e30
EXP_NEG_HALF = math.exp(-0.5)
ATTN_GROUP = 4

assert PAIR == LANES


def _rms_norm(x, g):
    return x * lax.rsqrt(jnp.mean(x * x, axis=-1, keepdims=True) + NORM_EPS) * g


def _resident(shape):
    return pl.BlockSpec(shape, lambda *_: (0,) * len(shape), pipeline_mode=pl.Buffered(1))


def _inproj_kernel(x_ref, g_ref, w_ref, mu_ref, gb_ref, zs_ref, qkv_ref, gate_ref, carry_ref,
                   *, blocks_per_seq, rwkv_cols, attn_cols):
    i = pl.program_id(0)
    rows = x_ref.shape[0]
    hb = _rms_norm(x_ref[...], g_ref[...]).astype(BF16)

    z = jnp.dot(hb, w_ref[:, :rwkv_cols], preferred_element_type=F32)

    @pl.when(i % blocks_per_seq == 0)
    def _():
        carry_ref[...] = jnp.zeros_like(carry_ref)

    row = lax.broadcasted_iota(jnp.int32, z.shape, 0)
    prev = jnp.where(row == 0, carry_ref[0:1, :], pltpu.roll(z, 1, 0))
    zs_ref[...] = z + (prev - z) * mu_ref[...]
    carry_ref[0:1, :] = z[rows - 1:rows, :]

    qkv_ref[...] = jnp.dot(hb, w_ref[:, rwkv_cols:rwkv_cols + attn_cols],
                           preferred_element_type=F32).astype(BF16)
    zg = jnp.dot(hb, w_ref[:, rwkv_cols + attn_cols:], preferred_element_type=F32)
    gate_ref[...] = jax.nn.sigmoid(zg + gb_ref[...]).astype(BF16)


def _inproj(x2, g, w_bf, mu, gate_b, *, seq, rwkv_cols, attn_cols, tm):
    m, d = x2.shape
    gate_cols = w_bf.shape[1] - rwkv_cols - attn_cols
    body = functools.partial(_inproj_kernel, blocks_per_seq=seq // tm,
                             rwkv_cols=rwkv_cols, attn_cols=attn_cols)
    return pl.pallas_call(
        body,
        grid=(m // tm,),
        in_specs=[
            pl.BlockSpec((tm, d), lambda i: (i, 0)),
            _resident((1, d)),
            _resident(w_bf.shape),
            _resident((1, rwkv_cols)),
            _resident((1, gate_cols)),
        ],
        out_specs=[
            pl.BlockSpec((tm, rwkv_cols), lambda i: (i, 0)),
            pl.BlockSpec((tm, attn_cols), lambda i: (i, 0)),
            pl.BlockSpec((tm, gate_cols), lambda i: (i, 0)),
        ],
        out_shape=[
            jax.ShapeDtypeStruct((m, rwkv_cols), F32),
            jax.ShapeDtypeStruct((m, attn_cols), BF16),
            jax.ShapeDtypeStruct((m, gate_cols), BF16),
        ],
        scratch_shapes=[pltpu.VMEM((SUBLANES, rwkv_cols), F32)],
        compiler_params=pltpu.CompilerParams(
            dimension_semantics=("arbitrary",), vmem_limit_bytes=VMEM_LIMIT_BYTES),
        name="inproj",
    )(x2, g, w_bf, mu, gate_b)


def _mm(a, b, dims=None):
    a = a.astype(BF16)
    b = b.astype(BF16)
    if dims is None:
        return jnp.dot(a, b, preferred_element_type=F32)
    return lax.dot_general(a, b, (dims, ((), ())), preferred_element_type=F32)


_NT = ((1,), (1,))
_TN = ((0,), (0,))


def _split2(x):
    hi = x.astype(BF16)
    return hi, (x - hi.astype(F32)).astype(BF16)


def _split3(x):
    hi = x.astype(BF16)
    r1 = x - hi.astype(F32)
    mid = r1.astype(BF16)
    return hi, mid, (r1 - mid.astype(F32)).astype(BF16)


def _unit_lower_inverse_minus_eye(mats, blk16, blk32, same_block):
    n = same_block.shape[0] // CHUNK

    def mul(xs, ys):
        out = []
        for x, y in zip(xs, ys):
            y_diag = jnp.where(same_block, jnp.concatenate([y.astype(BF16)] * n, axis=0),
                               jnp.zeros((), BF16))
            out.append(jnp.dot(x.astype(BF16), y_diag, preferred_element_type=F32))
        return out

    ad = [jnp.where(blk16, a, 0.0) for a in mats]
    a2 = mul(ad, ad)
    a4 = mul(a2, a2)
    p1 = mul(ad, a2)
    a8 = mul(a4, a4)
    p2 = mul(a4, a8)
    x1 = [b - a - p for a, b, p in zip(ad, a2, p1)]
    x2 = [a + b + p for a, b, p in zip(a4, a8, p2)]
    p3 = mul(x1, x2)
    m = [a + b + p for a, b, p in zip(x1, x2, p3)]
    for mask in (jnp.logical_and(blk32, jnp.logical_not(blk16)), jnp.logical_not(blk32)):
        off = [jnp.where(mask, a, 0.0) for a in mats]
        z = [o + p for o, p in zip(off, mul(m, off))]
        m = [mi - zi - p for mi, zi, p in zip(m, z, mul(z, m))]
    return m


def _rwkv_kernel(r_ref, k_ref, v_ref, wa_ref, gd_ref,
                 w0_ref, w2_ref, a0_ref, a2_ref, g2_ref, kk_ref, ka_ref, rk_ref, lnw_ref, lnb_ref,
                 o_ref, state_ref):
    tb = r_ref.shape[0]
    n_chunks = tb // CHUNK

    @pl.when(pl.program_id(2) == 0)
    def _():
        state_ref[...] = jnp.zeros_like(state_ref)

    lane = lax.broadcasted_iota(jnp.int32, (1, PAIR), 1)
    head0 = lane < HEAD_DIM
    head_masks = (head0, jnp.logical_not(head0))
    rr = lax.broadcasted_iota(jnp.int32, (PAIR, PAIR), 0)
    cc = lax.broadcasted_iota(jnp.int32, (PAIR, PAIR), 1)
    same_head = (rr // HEAD_DIM) == (cc // HEAD_DIM)
    ones2 = jnp.concatenate([same_head.astype(BF16)] * 2, axis=0)

    def head_sum(t):
        return jnp.dot(jnp.concatenate(_split2(t), axis=1), ones2, preferred_element_type=F32)

    r = r_ref[...]
    k = k_ref[...]
    v = v_ref[...]
    wa = wa_ref[...]
    u = w0_ref[...] + _mm(jnp.tanh(wa), w2_ref[...])
    ld = -EXP_NEG_HALF * jax.nn.sigmoid(u)
    a = jax.nn.sigmoid(a0_ref[...] + _mm(wa, a2_ref[...]))
    g = _mm(jax.nn.sigmoid(gd_ref[...]), g2_ref[...])
    kk = k * kk_ref[...]
    kk = kk / jnp.maximum(jnp.sqrt(head_sum(kk * kk)), 1e-12)
    kp = k * (1.0 + (a - 1.0) * ka_ref[...])
    bonus = head_sum(r * kp * rk_ref[...]) * v

    wide = 2 * PAIR
    ti = lax.broadcasted_iota(jnp.int32, (CHUNK, wide), 0)
    si = lax.broadcasted_iota(jnp.int32, (CHUNK, wide), 1) % CHUNK
    lower_w = si <= ti
    strict_w = si < ti
    blk16_w = (ti // 16) == (si // 16)
    blk32_w = (ti // 32) == (si // 32)
    bi = lax.broadcasted_iota(jnp.int32, (wide, wide), 0)
    bj = lax.broadcasted_iota(jnp.int32, (wide, wide), 1)
    same_block = (bi // CHUNK) == (bj // CHUNK)
    head0_w = jnp.concatenate([head0, head0], axis=1)
    tri3 = jnp.concatenate([lower_w[:, :CHUNK].astype(BF16)] * 3, axis=1)

    def per_head(t):
        mask = head0 if t.shape[1] == PAIR else head0_w
        return jnp.concatenate([jnp.where(mask, t, 0.0), jnp.where(mask, 0.0, t)], axis=0)

    ld_parts = _split3(ld)
    cum = jnp.concatenate([
        jnp.dot(tri3, jnp.concatenate([part[c * CHUNK:(c + 1) * CHUNK] for part in ld_parts], axis=0),
                preferred_element_type=F32)
        for c in range(n_chunks)], axis=0)
    e_pos = jnp.exp(cum)
    e_neg = jnp.exp(-cum)
    rt_all = r * e_pos
    kt_all = kk * jnp.exp(cum - ld)
    bt_all = kk * a * e_neg
    kpt_all = kp * e_neg

    chunks = range(n_chunks)
    rows = [slice(c * CHUNK, (c + 1) * CHUNK) for c in chunks]
    rt = [rt_all[s] for s in rows]
    kt = [kt_all[s] for s in rows]
    bt = [bt_all[s] for s in rows]
    kpt = [kpt_all[s] for s in rows]
    vc = [v[s] for s in rows]
    gamma_end = [e_pos[s][CHUNK - 1:CHUNK, :] for s in rows]
    zeros = jnp.zeros((CHUNK, PAIR), F32)

    sc = [_mm(jnp.concatenate([kt[c], rt[c]], axis=0),
              jnp.concatenate([per_head(bt[c]), per_head(kpt[c])], axis=0), _NT) for c in chunks]
    a_bk = [jnp.where(strict_w, s[:CHUNK], 0.0) for s in sc]
    r_bk = [jnp.where(lower_w, s[CHUNK:], 0.0) for s in sc]
    akv = [_mm(a_bk[c][:, PAIR:], per_head(vc[c])) for c in chunks]
    t_m = _unit_lower_inverse_minus_eye(
        [jnp.concatenate([a_bk[c][:, :PAIR], a_bk[c + 1][:, :PAIR]], axis=1)
         for c in range(0, n_chunks, 2)], blk16_w, blk32_w, same_block)
    t_m = [t_m[c // 2][:, (c % 2) * PAIR:(c % 2 + 1) * PAIR] for c in chunks]
    x = [jnp.concatenate([akv[c], kt[c]], axis=1) for c in chunks]
    fe = [x[c] + _mm(t_m[c], per_head(x[c])) for c in chunks]
    f_all = [t[:, :PAIR] for t in fe]
    e_all = [t[:, PAIR:] for t in fe]
    yy = [_mm(r_bk[c], jnp.concatenate([per_head(-fe[c]),
                                        per_head(jnp.concatenate([vc[c], zeros], axis=1))], axis=0))
          for c in chunks]
    y_own = [t[:, :PAIR] for t in yy]
    ry = [rt[c] + yy[c][:, PAIR:] for c in chunks]
    b_e = [jnp.where(same_head, _mm(bt[c], e_all[c], _TN), 0.0) for c in chunks]
    q_g = [jnp.where(same_head, _mm(jnp.concatenate([vc[c], -f_all[c]], axis=0),
                                    jnp.concatenate([kpt[c], bt[c]], axis=0), _TN), 0.0)
           for c in chunks]

    state = state_ref[...]
    y_chunks = []
    for c in chunks:
        y_chunks.append(_mm(ry[c], state, _NT) + y_own[c])
        state = (state - _mm(state, b_e[c], _NT) + q_g[c]) * gamma_end[c]
    state_ref[...] = state

    y = jnp.concatenate(y_chunks, axis=0)
    mean = head_sum(y) * (1.0 / HEAD_DIM)
    yc = y - mean
    var = head_sum(yc * yc) * (1.0 / HEAD_DIM)
    yn = yc * lax.rsqrt(var + GN_EPS) * lnw_ref[...] + lnb_ref[...]
    o_ref[...] = ((yn + bonus) * g).astype(o_ref.dtype)


def _rwkv(zs3, w0, w2p, a0, a2p, g2, k_k, k_a, r_k, lnx_w, lnx_b, *, width, tb):
    nb, seq, _ = zs3.shape
    n_pairs = width // PAIR
    lora_tile = 3 * n_pairs
    gate_tile = 3 * n_pairs + 1

    def tok(col_tile):
        return pl.BlockSpec((None, tb, PAIR), lambda b, p, t: (b, t, col_tile(p)))

    def par(rows):
        return pl.BlockSpec((rows, PAIR), lambda b, p, t: (0, p))

    return pl.pallas_call(
        _rwkv_kernel,
        grid=(nb, n_pairs, seq // tb),
        in_specs=[
            tok(lambda p: p), tok(lambda p: n_pairs + p), tok(lambda p: 2 * n_pairs + p),
            tok(lambda p: lora_tile), tok(lambda p: gate_tile),
            par(1), par(PAIR), par(1), par(PAIR), par(GATE_RANK),
            par(1), par(1), par(1), par(1), par(1),
        ],
        out_specs=pl.BlockSpec((None, tb, PAIR), lambda b, p, t: (b, t, p)),
        out_shape=jax.ShapeDtypeStruct((nb, seq, width), BF16),
        scratch_shapes=[pltpu.VMEM((PAIR, PAIR), F32)],
        compiler_params=pltpu.CompilerParams(
            dimension_semantics=("parallel", "parallel", "arbitrary"),
            vmem_limit_bytes=VMEM_LIMIT_BYTES),
        name="rwkv",
    )(zs3, zs3, zs3, zs3, zs3, w0, w2p, a0, a2p, g2, k_k, k_a, r_k, lnx_w, lnx_b)


def _attn_kernel(q_ref, k_ref, v_ref, bias_ref, o_ref, kpad_ref, vpad_ref, *, scale, group):
    tq = q_ref.shape[0]
    seq = k_ref.shape[0]
    t = pl.program_id(2)

    @pl.when(t == 0)
    def _():
        zeros = jnp.zeros((LEFT_PAD, PAIR), kpad_ref.dtype)
        kpad_ref[0:LEFT_PAD, :] = zeros
        vpad_ref[0:LEFT_PAD, :] = zeros
        kpad_ref[LEFT_PAD:LEFT_PAD + seq, :] = k_ref[...]
        vpad_ref[LEFT_PAD:LEFT_PAD + seq, :] = v_ref[...]

    lane = lax.broadcasted_iota(jnp.int32, (1, PAIR), 1)
    head0 = lane < HEAD_DIM
    head_masks = (head0, jnp.logical_not(head0))
    col = lax.broadcasted_iota(jnp.int32, (CHUNK, BAND), 1)

    def group_body(g, first, masked):
        q, kb, vb = [], [], []
        for j in range(group):
            rows = pl.ds(pl.multiple_of((g * group + j) * CHUNK, CHUNK), CHUNK)
            band = pl.ds(pl.multiple_of((first + j) * CHUNK, CHUNK), BAND)
            q.append(q_ref[rows, :] * scale)
            kb.append(kpad_ref[band, :])
            vb.append(vpad_ref[band, :])
        inst = [(j, h) for j in range(group) for h in range(2)]
        s = [lax.dot_general(jnp.where(head_masks[h], q[j], jnp.zeros_like(q[j])), kb[j],
                             (((1,), (1,)), ((), ())), preferred_element_type=F32) + bias_ref[h]
             for j, h in inst]
        if masked:
            s = [jnp.where(col >= (LEFT_CHUNKS - first - j) * CHUNK, si, NEG_INF)
                 for si, (j, _) in zip(s, inst)]
        p = [jnp.exp(si - jnp.max(si, axis=-1, keepdims=True)) for si in s]
        denom = [jnp.sum(pi, axis=-1, keepdims=True) for pi in p]
        o = [jnp.dot(pi.astype(vb[j].dtype), vb[j], preferred_element_type=F32) / di
             for pi, di, (j, _) in zip(p, denom, inst)]
        for j in range(group):
            rows = pl.ds(pl.multiple_of((g * group + j) * CHUNK, CHUNK), CHUNK)
            o_ref[rows, :] = jnp.where(head0, o[2 * j], o[2 * j + 1]).astype(o_ref.dtype)

    def group_step(g, carry):
        first = t * (tq // CHUNK) + g * group
        lax.cond(first < LEFT_CHUNKS,
                 lambda: group_body(g, first, True), lambda: group_body(g, first, False))
        return carry

    lax.fori_loop(0, tq // (CHUNK * group), group_step, 0)


def _attention(qkv3, bias, *, width, tq):
    nb, seq, _ = qkv3.shape
    n_pairs = width // PAIR
    scale = HEAD_DIM ** -0.5
    assert math.frexp(scale)[0] == 0.5, "folding the scale into bf16 q is exact only for a power of two"
    return pl.pallas_call(
        functools.partial(_attn_kernel, scale=scale, group=ATTN_GROUP),
        grid=(nb, n_pairs, seq // tq),
        in_specs=[
            pl.BlockSpec((None, tq, PAIR), lambda b, p, t: (b, t, p)),
            pl.BlockSpec((None, seq, PAIR), lambda b, p, t: (b, 0, n_pairs + p)),
            pl.BlockSpec((None, seq, PAIR), lambda b, p, t: (b, 0, 2 * n_pairs + p)),
            pl.BlockSpec((2, CHUNK, BAND), lambda b, p, t: (p, 0, 0)),
        ],
        out_specs=pl.BlockSpec((None, tq, PAIR), lambda b, p, t: (b, t, p)),
        out_shape=jax.ShapeDtypeStruct((nb, seq, width), BF16),
        scratch_shapes=[pltpu.VMEM((LEFT_PAD + seq, PAIR), BF16)] * 2,
        compiler_params=pltpu.CompilerParams(
            dimension_semantics=("parallel", "parallel", "arbitrary"),
            vmem_limit_bytes=VMEM_LIMIT_BYTES),
        name="chunk_attention",
    )(qkv3, qkv3, qkv3, bias)


def _band_bias(rel_bias):
    period = BAND + CHUNK
    n = jnp.arange(period)
    j_minus_q = jnp.where(n < BAND, n, n - period)
    idx = jnp.clip(LEFT_PAD - j_minus_q, -(CHUNK - 1), MAX_REL) + (CHUNK - 1)
    diagonals = rel_bias.astype(F32)[:, idx]
    skewed = jnp.tile(diagonals, (1, CHUNK))[:, :CHUNK * (period - 1)]
    return skewed.reshape(-1, CHUNK, period - 1)[:, :, :BAND]


def _merge_kernel(x_ref, oa_ref, ob_ref, gate_ref, wa_ref, wb_ref, wo_ref, o_ref):
    d = x_ref.shape[1]
    br_a = jnp.dot(oa_ref[...], wa_ref[...], preferred_element_type=F32)
    br_b = jnp.dot(ob_ref[...], wb_ref[...], preferred_element_type=F32)
    merged = gate_ref[:, :d].astype(F32) * br_a + gate_ref[:, d:].astype(F32) * br_b
    o_ref[...] = x_ref[...] + jnp.dot(merged.astype(BF16), wo_ref[...], preferred_element_type=F32)


def _merge(x2, oa, ob, gates, w_a, w_b, w_o, *, tm):
    m, d = x2.shape
    row = lambda cols: pl.BlockSpec((tm, cols), lambda i: (i, 0))
    return pl.pallas_call(
        _merge_kernel,
        grid=(m // tm,),
        in_specs=[row(d), row(oa.shape[1]), row(ob.shape[1]), row(gates.shape[1]),
                  _resident(w_a.shape), _resident(w_b.shape), _resident(w_o.shape)],
        out_specs=row(d),
        out_shape=jax.ShapeDtypeStruct((m, d), F32),
        compiler_params=pltpu.CompilerParams(
            dimension_semantics=("parallel",), vmem_limit_bytes=VMEM_LIMIT_BYTES),
        name="merge",
    )(x2, oa, ob, gates, w_a, w_b, w_o)


def _ffn_kernel(x_ref, p_ref, g2_ref, wup_ref, cw_ref, cb_ref, wdn_ref, g3_ref, wple_ref, wpg_ref,
                bpg_ref, gf_ref, o_ref, carry_ref, *, blocks_per_seq, d_ff, ff_tile, final_norm):
    i = pl.program_id(0)
    rows = x_ref.shape[0]
    x = x_ref[...]
    hb = _rms_norm(x, g2_ref[...]).astype(BF16)

    @pl.when(i % blocks_per_seq == 0)
    def _():
        carry_ref[...] = jnp.zeros_like(carry_ref)

    row = lax.broadcasted_iota(jnp.int32, (rows, ff_tile), 0)
    acc = x
    for j in range(d_ff // ff_tile):
        cols = slice(j * ff_tile, (j + 1) * ff_tile)
        a = jnp.dot(hb, wup_ref[:, cols], preferred_element_type=F32)
        gv = jnp.dot(hb, wup_ref[:, d_ff + j * ff_tile:d_ff + (j + 1) * ff_tile],
                     preferred_element_type=F32)
        c0 = carry_ref[0:1, cols]
        c1 = carry_ref[1:2, cols]
        prev1 = jnp.where(row == 0, c1, pltpu.roll(a, 1, 0))
        prev2 = jnp.where(row == 0, c0, jnp.where(row == 1, c1, pltpu.roll(a, 2, 0)))
        carry_ref[0:2, cols] = a[rows - 2:rows, :]
        conv = (cw_ref[0:1, cols] * prev2 + cw_ref[1:2, cols] * prev1 + cw_ref[2:3, cols] * a
                + cb_ref[:, cols])
        act = 0.5 * conv * (1.0 + lax.erf(conv * math.sqrt(0.5))) * gv
        acc = acc + jnp.dot(act.astype(BF16), wdn_ref[cols, :], preferred_element_type=F32)

    h3 = _rms_norm(acc, g3_ref[...]).astype(BF16)
    pg = jax.nn.sigmoid(jnp.dot(h3, wpg_ref[...], preferred_element_type=F32) + bpg_ref[...])
    ple = jnp.dot(p_ref[...].astype(BF16), wple_ref[...], preferred_element_type=F32)
    out = acc + ple * pg
    if final_norm:
        out = _rms_norm(out, gf_ref[...])
    o_ref[...] = out


def _ffn(x2, p2, g2, w_up, conv_w, conv_b, w_down, g3, w_ple, w_pg, b_pg, gf, *, seq, tm,
         final_norm):
    m, d = x2.shape
    d_ff = w_down.shape[0]
    ff_tile = d_ff // 2 if (d_ff // 2) % LANES == 0 else d_ff
    body = functools.partial(_ffn_kernel, blocks_per_seq=seq // tm, d_ff=d_ff, ff_tile=ff_tile,
                             final_norm=final_norm)
    row = lambda cols: pl.BlockSpec((tm, cols), lambda i: (i, 0))
    return pl.pallas_call(
        body,
        grid=(m // tm,),
        in_specs=[row(d), row(p2.shape[1]), _resident((1, d)), _resident(w_up.shape),
                  _resident(conv_w.shape), _resident((1, d_ff)), _resident(w_down.shape),
                  _resident((1, d)), _resident(w_ple.shape), _resident(w_pg.shape),
                  _resident((1, d)), _resident((1, d))],
        out_specs=row(d),
        out_shape=jax.ShapeDtypeStruct((m, d), F32),
        scratch_shapes=[pltpu.VMEM((SUBLANES, d_ff), F32)],
        compiler_params=pltpu.CompilerParams(
            dimension_semantics=("arbitrary",), vmem_limit_bytes=VMEM_LIMIT_BYTES),
        name="ffn",
    )(x2, p2, g2, w_up, conv_w, conv_b, w_down, g3, w_ple, w_pg, b_pg, gf)


def kernel(x, p, ln1_g, w_in, mix_mu, w0, w2, a0, a2, g2, k_k, k_a, r_k, lnx_w, lnx_b, rel_bias, gate_b, w_br_rwkv, w_br_attn, w_o, ln2_g, w_ffn_up, conv_w, conv_b, w_ffn_down, ln3_g, w_ple, w_pg, b_pg, lnf_g):
    nb, seq, d = x.shape
    depth = w_in.shape[0]
    rwkv_width = w_br_rwkv.shape[1]
    attn_width = w_br_attn.shape[1]
    rwkv_cols = 3 * rwkv_width + DECAY_RANK + ICLR_RANK + GATE_RANK
    attn_cols = 3 * attn_width
    assert DECAY_RANK + ICLR_RANK == LANES and GATE_RANK == LANES
    assert seq % 512 == 0 and seq >= BAND
    row = lambda t: t.reshape(1, -1)
    m = nb * seq
    x2 = x.reshape(m, d)

    for i in range(depth):
        last = i == depth - 1
        zs, qkv, gates = _inproj(
            x2, row(ln1_g[i]), w_in[i].astype(BF16), row(mix_mu[i]), row(gate_b[i]),
            seq=seq, rwkv_cols=rwkv_cols, attn_cols=attn_cols, tm=512)

        w2p = jnp.concatenate([w2[i], jnp.zeros_like(a2[i])], axis=0)
        a2p = jnp.concatenate([jnp.zeros_like(w2[i]), a2[i]], axis=0)
        o_a = _rwkv(zs.reshape(nb, seq, rwkv_cols), row(w0[i]), w2p, row(a0[i]), a2p, g2[i],
                    row(k_k[i]), row(k_a[i]), row(r_k[i]), row(lnx_w[i]), row(lnx_b[i]),
                    width=rwkv_width, tb=1024)

        o_b = _attention(qkv.reshape(nb, seq, attn_cols), _band_bias(rel_bias[i]),
                         width=attn_width, tq=512)

        x2 = _merge(x2, o_a.reshape(m, rwkv_width), o_b.reshape(m, attn_width), gates,
                    w_br_rwkv[i].astype(BF16), w_br_attn[i].astype(BF16), w_o[i].astype(BF16),
                    tm=512)

        x2 = _ffn(x2, p[i].reshape(m, -1), row(ln2_g[i]), w_ffn_up[i].astype(BF16), conv_w[i],
                  row(conv_b[i]), w_ffn_down[i].astype(BF16), row(ln3_g[i]),
                  w_ple[i].astype(BF16), w_pg[i].astype(BF16), row(b_pg[i]), row(lnf_g),
                  seq=seq, tm=512, final_norm=last)
    return x2.reshape(nb, seq, d)
```

```python
import functools
import math

import jax
import jax.numpy as jnp
from jax import lax
from jax.experimental import pallas as pl
from jax.experimental.pallas import tpu as pltpu

F32 = jnp.float32
BF16 = jnp.bfloat16

LANES = 128
SUBLANES = 8
VMEM_LIMIT_BYTES = 56 * 1024 * 1024

HEAD_DIM = 64
PAIR = 2 * HEAD_DIM
CHUNK = 64
LEFT_CHUNKS = 8
BAND = (LEFT_CHUNKS + 1) * CHUNK
LEFT_PAD = LEFT_CHUNKS * CHUNK
MAX_REL = 128
DECAY_RANK = 64
ICLR_RANK = 64
GATE_RANK = 128
CONV_W = 3
NORM_EPS = 1e-6
GN_EPS = 64e-5
NEG_INF = -1e30
EXP_NEG_HALF = math.exp(-0.5)
ATTN_GROUP = 4

assert PAIR == LANES


def _rms_norm(x, g):
    return x * lax.rsqrt(jnp.mean(x * x, axis=-1, keepdims=True) + NORM_EPS) * g


def _resident(shape):
    return pl.BlockSpec(shape, lambda *_: (0,) * len(shape), pipeline_mode=pl.Buffered(1))


def _inproj_kernel(x_ref, g_ref, w_ref, mu_ref, gb_ref, zs_ref, qkv_ref, gate_ref, carry_ref,
                   *, blocks_per_seq, rwkv_cols, attn_cols):
    i = pl.program_id(0)
    rows = x_ref.shape[0]

    @pl.when(i % blocks_per_seq == 0)
    def _():
        carry_ref[...] = jnp.zeros_like(carry_ref)

    hb = _rms_norm(x_ref[...], g_ref[...]).astype(BF16)

    zg = jnp.dot(hb, w_ref[:, rwkv_cols + attn_cols:], preferred_element_type=F32)
    gate_ref[...] = jax.nn.sigmoid(zg + gb_ref[...]).astype(BF16)

    z = jnp.dot(hb, w_ref[:, :rwkv_cols], preferred_element_type=F32)
    row = lax.broadcasted_iota(jnp.int32, z.shape, 0)
    prev = jnp.where(row == 0, carry_ref[0:1, :], pltpu.roll(z, 1, 0))
    zs_ref[...] = z + (prev - z) * mu_ref[...]
    carry_ref[0:1, :] = z[rows - 1:rows, :]

    qkv_ref[...] = jnp.dot(hb, w_ref[:, rwkv_cols:rwkv_cols + attn_cols],
                           preferred_element_type=F32).astype(BF16)


def _inproj(x2, g, w_bf, mu, gate_b, *, seq, rwkv_cols, attn_cols, tm):
    m, d = x2.shape
    gate_cols = w_bf.shape[1] - rwkv_cols - attn_cols
    body = functools.partial(_inproj_kernel, blocks_per_seq=seq // tm,
                             rwkv_cols=rwkv_cols, attn_cols=attn_cols)
    return pl.pallas_call(
        body,
        grid=(m // tm,),
        in_specs=[
            pl.BlockSpec((tm, d), lambda i: (i, 0)),
            _resident((1, d)),
            _resident(w_bf.shape),
            _resident((1, rwkv_cols)),
            _resident((1, gate_cols)),
        ],
        out_specs=[
            pl.BlockSpec((tm, rwkv_cols), lambda i: (i, 0)),
            pl.BlockSpec((tm, attn_cols), lambda i: (i, 0)),
            pl.BlockSpec((tm, gate_cols), lambda i: (i, 0)),
        ],
        out_shape=[
            jax.ShapeDtypeStruct((m, rwkv_cols), F32),
            jax.ShapeDtypeStruct((m, attn_cols), BF16),
            jax.ShapeDtypeStruct((m, gate_cols), BF16),
        ],
        scratch_shapes=[pltpu.VMEM((SUBLANES, rwkv_cols), F32)],
        compiler_params=pltpu.CompilerParams(
            dimension_semantics=("arbitrary",), vmem_limit_bytes=VMEM_LIMIT_BYTES),
        name="inproj",
    )(x2, g, w_bf, mu, gate_b)


def _mm(a, b, dims=None):
    a = a.astype(BF16)
    b = b.astype(BF16)
    if dims is None:
        return jnp.dot(a, b, preferred_element_type=F32)
    return lax.dot_general(a, b, (dims, ((), ())), preferred_element_type=F32)


_NT = ((1,), (1,))
_TN = ((0,), (0,))


def _split2(x):
    hi = x.astype(BF16)
    return hi, (x - hi.astype(F32)).astype(BF16)


def _split3(x):
    hi = x.astype(BF16)
    r1 = x - hi.astype(F32)
    mid = r1.astype(BF16)
    return hi, mid, (r1 - mid.astype(F32)).astype(BF16)


def _unit_lower_inverse_minus_eye(mats, blk16, blk32, same_block):
    n = same_block.shape[0] // CHUNK

    def mul(xs, ys):
        out = []
        for x, y in zip(xs, ys):
            y_diag = jnp.where(same_block, jnp.concatenate([y.astype(BF16)] * n, axis=0),
                               jnp.zeros((), BF16))
            out.append(jnp.dot(x.astype(BF16), y_diag, preferred_element_type=F32))
        return out

    ad = [jnp.where(blk16, a, 0.0) for a in mats]
    a2 = mul(ad, ad)
    a4 = mul(a2, a2)
    p1 = mul(ad, a2)
    a8 = mul(a4, a4)
    p2 = mul(a4, a8)
    x1 = [b - a - p for a, b, p in zip(ad, a2, p1)]
    x2 = [a + b + p for a, b, p in zip(a4, a8, p2)]
    p3 = mul(x1, x2)
    m = [a + b + p for a, b, p in zip(x1, x2, p3)]
    for mask in (jnp.logical_and(blk32, jnp.logical_not(blk16)), jnp.logical_not(blk32)):
        off = [jnp.where(mask, a, 0.0) for a in mats]
        z = [o + p for o, p in zip(off, mul(m, off))]
        m = [mi - zi - p for mi, zi, p in zip(m, z, mul(z, m))]
    return m


def _rwkv_kernel(r_ref, k_ref, v_ref, wa_ref, gd_ref,
                 w0_ref, w2_ref, a0_ref, a2_ref, g2_ref, kk_ref, ka_ref, rk_ref, lnw_ref, lnb_ref,
                 o_ref, state_ref):
    tb, width = r_ref.shape
    n_chunks = tb // CHUNK
    pairs = range(width // PAIR)
    lanes = [slice(q * PAIR, (q + 1) * PAIR) for q in pairs]

    @pl.when(pl.program_id(1) == 0)
    def _():
        state_ref[...] = jnp.zeros_like(state_ref)

    lane = lax.broadcasted_iota(jnp.int32, (1, PAIR), 1)
    head0 = lane < HEAD_DIM
    rr = lax.broadcasted_iota(jnp.int32, (PAIR, PAIR), 0)
    cc = lax.broadcasted_iota(jnp.int32, (PAIR, PAIR), 1)
    same_head = (rr // HEAD_DIM) == (cc // HEAD_DIM)
    ones2 = jnp.concatenate([same_head.astype(BF16)] * 2, axis=0)

    def head_sum(t):
        return jnp.concatenate(
            [jnp.dot(jnp.concatenate(_split2(t[:, l]), axis=1), ones2, preferred_element_type=F32)
             for l in lanes], axis=1)

    r = r_ref[...]
    k = k_ref[...]
    v = v_ref[...]
    wa = wa_ref[...]
    u = w0_ref[...] + _mm(jnp.tanh(wa), w2_ref[...])
    ld = -EXP_NEG_HALF * jax.nn.sigmoid(u)
    a = jax.nn.sigmoid(a0_ref[...] + _mm(wa, a2_ref[...]))
    g = _mm(jax.nn.sigmoid(gd_ref[...]), g2_ref[...])
    kk = k * kk_ref[...]
    kk = kk / jnp.maximum(jnp.sqrt(head_sum(kk * kk)), 1e-12)
    kp = k * (1.0 + (a - 1.0) * ka_ref[...])
    bonus = head_sum(r * kp * rk_ref[...]) * v

    wide = 2 * PAIR
    ti = lax.broadcasted_iota(jnp.int32, (CHUNK, wide), 0)
    si = lax.broadcasted_iota(jnp.int32, (CHUNK, wide), 1) % CHUNK
    lower_w = si <= ti
    strict_w = si < ti
    blk16_w = (ti // 16) == (si // 16)
    blk32_w = (ti // 32) == (si // 32)
    bi = lax.broadcasted_iota(jnp.int32, (wide, wide), 0)
    bj = lax.broadcasted_iota(jnp.int32, (wide, wide), 1)
    same_block = (bi // CHUNK) == (bj // CHUNK)
    head0_w = jnp.concatenate([head0, head0], axis=1)
    tri3 = jnp.concatenate([lower_w[:, :CHUNK].astype(BF16)] * 3, axis=1)

    def per_head(t):
        mask = head0 if t.shape[1] == PAIR else head0_w
        return jnp.concatenate([jnp.where(mask, t, 0.0), jnp.where(mask, 0.0, t)], axis=0)

    ld_parts = _split3(ld)
    cum = jnp.concatenate([
        jnp.dot(tri3, jnp.concatenate([part[c * CHUNK:(c + 1) * CHUNK] for part in ld_parts], axis=0),
                preferred_element_type=F32)
        for c in range(n_chunks)], axis=0)
    e_pos = jnp.exp(cum)
    e_neg = jnp.exp(-cum)
    rt_all = r * e_pos
    kt_all = kk * jnp.exp(cum - ld)
    bt_all = kk * a * e_neg
    kpt_all = kp * e_neg

    items = [(q, c) for q in pairs for c in range(n_chunks)]
    n = range(len(items))
    cut = lambda t: [t[c * CHUNK:(c + 1) * CHUNK, lanes[q]] for q, c in items]
    rt, kt, bt, kpt, vc = cut(rt_all), cut(kt_all), cut(bt_all), cut(kpt_all), cut(v)
    gamma_end = [t[CHUNK - 1:CHUNK, :] for t in cut(e_pos)]
    zeros = jnp.zeros((CHUNK, PAIR), F32)

    sc = [_mm(jnp.concatenate([kt[i], rt[i]], axis=0),
              jnp.concatenate([per_head(bt[i]), per_head(kpt[i])], axis=0), _NT) for i in n]
    a_bk = [jnp.where(strict_w, s[:CHUNK], 0.0) for s in sc]
    r_bk = [jnp.where(lower_w, s[CHUNK:], 0.0) for s in sc]
    akv = [_mm(a_bk[i][:, PAIR:], per_head(vc[i])) for i in n]
    t_m = _unit_lower_inverse_minus_eye(
        [jnp.concatenate([a_bk[i][:, :PAIR], a_bk[i + 1][:, :PAIR]], axis=1)
         for i in range(0, len(items), 2)], blk16_w, blk32_w, same_block)
    t_m = [t_m[i // 2][:, (i % 2) * PAIR:(i % 2 + 1) * PAIR] for i in n]
    x = [jnp.concatenate([akv[i], kt[i]], axis=1) for i in n]
    fe = [x[i] + _mm(t_m[i], per_head(x[i])) for i in n]
    f_all = [t[:, :PAIR] for t in fe]
    e_all = [t[:, PAIR:] for t in fe]
    yy = [_mm(r_bk[i], jnp.concatenate([per_head(-fe[i]),
                                        per_head(jnp.concatenate([vc[i], zeros], axis=1))], axis=0))
          for i in n]
    y_own = [t[:, :PAIR] for t in yy]
    ry = [rt[i] + yy[i][:, PAIR:] for i in n]
    b_e = [jnp.where(same_head, _mm(bt[i], e_all[i], _TN), 0.0) for i in n]
    q_g = [jnp.where(same_head, _mm(jnp.concatenate([vc[i], -f_all[i]], axis=0),
                                    jnp.concatenate([kpt[i], bt[i]], axis=0), _TN), 0.0)
           for i in n]

    state = [state_ref[q] for q in pairs]
    y = [[None] * n_chunks for _ in pairs]
    for c in range(n_chunks):
        for q in pairs:
            i = q * n_chunks + c
            y[q][c] = _mm(ry[i], state[q], _NT) + y_own[i]
            state[q] = (state[q] - _mm(state[q], b_e[i], _NT) + q_g[i]) * gamma_end[i]
    for q in pairs:
        state_ref[q] = state[q]

    y = jnp.concatenate([jnp.concatenate(y[q], axis=0) for q in pairs], axis=1)
    mean = head_sum(y) * (1.0 / HEAD_DIM)
    yc = y - mean
    var = head_sum(yc * yc) * (1.0 / HEAD_DIM)
    yn = yc * lax.rsqrt(var + GN_EPS) * lnw_ref[...] + lnb_ref[...]
    o_ref[...] = ((yn + bonus) * g).astype(o_ref.dtype)


def _rwkv(zs3, w0, w2p, a0, a2p, g2, k_k, k_a, r_k, lnx_w, lnx_b, *, width, tb):
    nb, seq, _ = zs3.shape
    assert width % PAIR == 0 and (tb // CHUNK) % 2 == 0
    lora_tile = 3 * width // LANES
    gate_tile = lora_tile + 1

    def tok(cols, col_block):
        return pl.BlockSpec((None, tb, cols), lambda b, t: (b, t, col_block))

    return pl.pallas_call(
        _rwkv_kernel,
        grid=(nb, seq // tb),
        in_specs=[
            tok(width, 0), tok(width, 1), tok(width, 2), tok(LANES, lora_tile), tok(LANES, gate_tile),
            _resident(w0.shape), _resident(w2p.shape), _resident(a0.shape), _resident(a2p.shape),
            _resident(g2.shape), _resident(k_k.shape), _resident(k_a.shape), _resident(r_k.shape),
            _resident(lnx_w.shape), _resident(lnx_b.shape),
        ],
        out_specs=pl.BlockSpec((None, tb, width), lambda b, t: (b, t, 0)),
        out_shape=jax.ShapeDtypeStruct((nb, seq, width), BF16),
        scratch_shapes=[pltpu.VMEM((width // PAIR, PAIR, PAIR), F32)],
        compiler_params=pltpu.CompilerParams(
            dimension_semantics=("parallel", "arbitrary"), vmem_limit_bytes=VMEM_LIMIT_BYTES),
        name="rwkv",
    )(zs3, zs3, zs3, zs3, zs3, w0, w2p, a0, a2p, g2, k_k, k_a, r_k, lnx_w, lnx_b)


def _attn_kernel(q_ref, k_ref, v_ref, bias_ref, o_ref, kpad_ref, vpad_ref, *, scale, group):
    tq = q_ref.shape[0]
    seq = k_ref.shape[0]
    t = pl.program_id(2)

    @pl.when(t == 0)
    def _():
        zeros = jnp.zeros((LEFT_PAD, PAIR), kpad_ref.dtype)
        kpad_ref[0:LEFT_PAD, :] = zeros
        vpad_ref[0:LEFT_PAD, :] = zeros
        kpad_ref[LEFT_PAD:LEFT_PAD + seq, :] = k_ref[...]
        vpad_ref[LEFT_PAD:LEFT_PAD + seq, :] = v_ref[...]

    lane = lax.broadcasted_iota(jnp.int32, (1, PAIR), 1)
    head0 = lane < HEAD_DIM
    head_masks = (head0, jnp.logical_not(head0))
    col = lax.broadcasted_iota(jnp.int32, (CHUNK, BAND), 1)

    def group_body(g, first, masked):
        q, kb, vb = [], [], []
        for j in range(group):
            rows = pl.ds(pl.multiple_of((g * group + j) * CHUNK, CHUNK), CHUNK)
            band = pl.ds(pl.multiple_of((first + j) * CHUNK, CHUNK), BAND)
            q.append(q_ref[rows, :] * scale)
            kb.append(kpad_ref[band, :])
            vb.append(vpad_ref[band, :])
        inst = [(j, h) for j in range(group) for h in range(2)]
        s = [lax.dot_general(jnp.where(head_masks[h], q[j], jnp.zeros_like(q[j])), kb[j],
                             (((1,), (1,)), ((), ())), preferred_element_type=F32) + bias_ref[h]
             for j, h in inst]
        if masked:
            s = [jnp.where(col >= (LEFT_CHUNKS - first - j) * CHUNK, si, NEG_INF)
                 for si, (j, _) in zip(s, inst)]
        p = [jnp.exp(si - jnp.max(si, axis=-1, keepdims=True)) for si in s]
        denom = [jnp.sum(pi, axis=-1, keepdims=True) for pi in p]
        o = [jnp.dot(pi.astype(vb[j].dtype), vb[j], preferred_element_type=F32) / di
             for pi, di, (j, _) in zip(p, denom, inst)]
        for j in range(group):
            rows = pl.ds(pl.multiple_of((g * group + j) * CHUNK, CHUNK), CHUNK)
            o_ref[rows, :] = jnp.where(head0, o[2 * j], o[2 * j + 1]).astype(o_ref.dtype)

    def group_step(g, carry):
        first = t * (tq // CHUNK) + g * group
        lax.cond(first < LEFT_CHUNKS,
                 lambda: group_body(g, first, True), lambda: group_body(g, first, False))
        return carry

    lax.fori_loop(0, tq // (CHUNK * group), group_step, 0)


def _attention(qkv3, bias, *, width, tq):
    nb, seq, _ = qkv3.shape
    n_pairs = width // PAIR
    scale = HEAD_DIM ** -0.5
    assert math.frexp(scale)[0] == 0.5, "folding the scale into bf16 q is exact only for a power of two"
    return pl.pallas_call(
        functools.partial(_attn_kernel, scale=scale, group=ATTN_GROUP),
        grid=(nb, n_pairs, seq // tq),
        in_specs=[
            pl.BlockSpec((None, tq, PAIR), lambda b, p, t: (b, t, p)),
            pl.BlockSpec((None, seq, PAIR), lambda b, p, t: (b, 0, n_pairs + p)),
            pl.BlockSpec((None, seq, PAIR), lambda b, p, t: (b, 0, 2 * n_pairs + p)),
            pl.BlockSpec((2, CHUNK, BAND), lambda b, p, t: (p, 0, 0)),
        ],
        out_specs=pl.BlockSpec((None, tq, PAIR), lambda b, p, t: (b, t, p)),
        out_shape=jax.ShapeDtypeStruct((nb, seq, width), BF16),
        scratch_shapes=[pltpu.VMEM((LEFT_PAD + seq, PAIR), BF16)] * 2,
        compiler_params=pltpu.CompilerParams(
            dimension_semantics=("parallel", "parallel", "arbitrary"),
            vmem_limit_bytes=VMEM_LIMIT_BYTES),
        name="chunk_attention",
    )(qkv3, qkv3, qkv3, bias)


def _band_bias(rel_bias):
    period = BAND + CHUNK
    n = jnp.arange(period)
    j_minus_q = jnp.where(n < BAND, n, n - period)
    idx = jnp.clip(LEFT_PAD - j_minus_q, -(CHUNK - 1), MAX_REL) + (CHUNK - 1)
    diagonals = rel_bias.astype(F32)[:, idx]
    skewed = jnp.tile(diagonals, (1, CHUNK))[:, :CHUNK * (period - 1)]
    return skewed.reshape(-1, CHUNK, period - 1)[:, :, :BAND]


def _merge_kernel(x_ref, oa_ref, ob_ref, gate_ref, wa_ref, wb_ref, wo_ref, o_ref):
    d = x_ref.shape[1]
    br_a = jnp.dot(oa_ref[...], wa_ref[...], preferred_element_type=F32)
    br_b = jnp.dot(ob_ref[...], wb_ref[...], preferred_element_type=F32)
    merged = gate_ref[:, :d].astype(F32) * br_a + gate_ref[:, d:].astype(F32) * br_b
    o_ref[...] = x_ref[...] + jnp.dot(merged.astype(BF16), wo_ref[...], preferred_element_type=F32)


def _merge(x2, oa, ob, gates, w_a, w_b, w_o, *, tm):
    m, d = x2.shape
    row = lambda cols: pl.BlockSpec((tm, cols), lambda i: (i, 0))
    return pl.pallas_call(
        _merge_kernel,
        grid=(m // tm,),
        in_specs=[row(d), row(oa.shape[1]), row(ob.shape[1]), row(gates.shape[1]),
                  _resident(w_a.shape), _resident(w_b.shape), _resident(w_o.shape)],
        out_specs=row(d),
        out_shape=jax.ShapeDtypeStruct((m, d), F32),
        compiler_params=pltpu.CompilerParams(
            dimension_semantics=("parallel",), vmem_limit_bytes=VMEM_LIMIT_BYTES),
        name="merge",
    )(x2, oa, ob, gates, w_a, w_b, w_o)


def _ffn_kernel(x_ref, p_ref, g2_ref, wup_ref, cw_ref, cb_ref, wdn_ref, g3_ref, wple_ref, wpg_ref,
                bpg_ref, gf_ref, o_ref, carry_ref, *, blocks_per_seq, d_ff, ff_tile, final_norm):
    i = pl.program_id(0)
    rows = x_ref.shape[0]

    @pl.when(i % blocks_per_seq == 0)
    def _():
        carry_ref[...] = jnp.zeros_like(carry_ref)

    x = x_ref[...]
    hb = _rms_norm(x, g2_ref[...]).astype(BF16)
    row = lax.broadcasted_iota(jnp.int32, (rows, ff_tile), 0)
    acc = x
    for j in range(d_ff // ff_tile):
        cols = slice(j * ff_tile, (j + 1) * ff_tile)
        a = jnp.dot(hb, wup_ref[:, cols], preferred_element_type=F32)
        gv = jnp.dot(hb, wup_ref[:, d_ff + j * ff_tile:d_ff + (j + 1) * ff_tile],
                     preferred_element_type=F32)
        c0 = carry_ref[0:1, cols]
        c1 = carry_ref[1:2, cols]
        prev1 = jnp.where(row == 0, c1, pltpu.roll(a, 1, 0))
        prev2 = jnp.where(row == 0, c0, jnp.where(row == 1, c1, pltpu.roll(a, 2, 0)))
        carry_ref[0:2, cols] = a[rows - 2:rows, :]
        conv = (cw_ref[0:1, cols] * prev2 + cw_ref[1:2, cols] * prev1 + cw_ref[2:3, cols] * a
                + cb_ref[:, cols])
        act = 0.5 * conv * (1.0 + lax.erf(conv * math.sqrt(0.5))) * gv
        acc = acc + jnp.dot(act.astype(BF16), wdn_ref[cols, :], preferred_element_type=F32)

    h3 = _rms_norm(acc, g3_ref[...]).astype(BF16)
    pg = jax.nn.sigmoid(jnp.dot(h3, wpg_ref[...], preferred_element_type=F32) + bpg_ref[...])
    ple = jnp.dot(p_ref[...].astype(BF16), wple_ref[...], preferred_element_type=F32)
    out = acc + ple * pg
    if final_norm:
        out = _rms_norm(out, gf_ref[...])
    o_ref[...] = out


def _ffn(x2, p2, g2, w_up, conv_w, conv_b, w_down, g3, w_ple, w_pg, b_pg, gf, *, seq, tm,
         final_norm):
    m, d = x2.shape
    d_ff = w_down.shape[0]
    ff_tile = d_ff // 2 if (d_ff // 2) % LANES == 0 else d_ff
    body = functools.partial(_ffn_kernel, blocks_per_seq=seq // tm, d_ff=d_ff, ff_tile=ff_tile,
                             final_norm=final_norm)
    row = lambda cols: pl.BlockSpec((tm, cols), lambda i: (i, 0))
    return pl.pallas_call(
        body,
        grid=(m // tm,),
        in_specs=[row(d), row(p2.shape[1]), _resident((1, d)), _resident(w_up.shape),
                  _resident(conv_w.shape), _resident((1, d_ff)), _resident(w_down.shape),
                  _resident((1, d)), _resident(w_ple.shape), _resident(w_pg.shape),
                  _resident((1, d)), _resident((1, d))],
        out_specs=row(d),
        out_shape=jax.ShapeDtypeStruct((m, d), F32),
        scratch_shapes=[pltpu.VMEM((SUBLANES, d_ff), F32)],
        compiler_params=pltpu.CompilerParams(
            dimension_semantics=("arbitrary",), vmem_limit_bytes=VMEM_LIMIT_BYTES),
        name="ffn",
    )(x2, p2, g2, w_up, conv_w, conv_b, w_down, g3, w_ple, w_pg, b_pg, gf)


def kernel(x, p, ln1_g, w_in, mix_mu, w0, w2, a0, a2, g2, k_k, k_a, r_k, lnx_w, lnx_b, rel_bias, gate_b, w_br_rwkv, w_br_attn, w_o, ln2_g, w_ffn_up, conv_w, conv_b, w_ffn_down, ln3_g, w_ple, w_pg, b_pg, lnf_g):
    nb, seq, d = x.shape
    depth = w_in.shape[0]
    rwkv_width = w_br_rwkv.shape[1]
    attn_width = w_br_attn.shape[1]
    rwkv_cols = 3 * rwkv_width + DECAY_RANK + ICLR_RANK + GATE_RANK
    attn_cols = 3 * attn_width
    assert DECAY_RANK + ICLR_RANK == LANES and GATE_RANK == LANES
    assert seq % 512 == 0 and seq >= BAND
    row = lambda t: t.reshape(1, -1)
    m = nb * seq
    x2 = x.reshape(m, d)

    for i in range(depth):
        last = i == depth - 1
        zs, qkv, gates = _inproj(
            x2, row(ln1_g[i]), w_in[i].astype(BF16), row(mix_mu[i]), row(gate_b[i]),
            seq=seq, rwkv_cols=rwkv_cols, attn_cols=attn_cols, tm=512)

        w2p = jnp.concatenate([w2[i], jnp.zeros_like(a2[i])], axis=0)
        a2p = jnp.concatenate([jnp.zeros_like(w2[i]), a2[i]], axis=0)
        o_a = _rwkv(zs.reshape(nb, seq, rwkv_cols), row(w0[i]), w2p, row(a0[i]), a2p, g2[i],
                    row(k_k[i]), row(k_a[i]), row(r_k[i]), row(lnx_w[i]), row(lnx_b[i]),
                    width=rwkv_width, tb=512)

        o_b = _attention(qkv.reshape(nb, seq, attn_cols), _band_bias(rel_bias[i]),
                         width=attn_width, tq=512)

        x2 = _merge(x2, o_a.reshape(m, rwkv_width), o_b.reshape(m, attn_width), gates,
                    w_br_rwkv[i].astype(BF16), w_br_attn[i].astype(BF16), w_o[i].astype(BF16),
                    tm=512)

        x2 = _ffn(x2, p[i].reshape(m, -1), row(ln2_g[i]), w_ffn_up[i].astype(BF16), conv_w[i],
                  row(conv_b[i]), w_ffn_down[i].astype(BF16), row(ln3_g[i]),
                  w_ple[i].astype(BF16), w_pg[i].astype(BF16), row(b_pg[i]), row(lnf_g),
                  seq=seq, tm=512, final_norm=last)
    return x2.reshape(nb, seq, d)
```

```python
import functools
import math

import jax
import jax.numpy as jnp
from jax import lax
from jax.experimental import pallas as pl
from jax.experimental.pallas import tpu as pltpu

F32 = jnp.float32
BF16 = jnp.bfloat16

LANES = 128
SUBLANES = 8
MXU_WIDTH = 256
VMEM_LIMIT_BYTES = 56 * 1024 * 1024

HEAD_DIM = 64
PAIR = 2 * HEAD_DIM
CHUNK = 64
LEFT_CHUNKS = 8
BAND = (LEFT_CHUNKS + 1) * CHUNK
LEFT_PAD = LEFT_CHUNKS * CHUNK
MAX_REL = 128
DECAY_RANK = 64
ICLR_RANK = 64
GATE_RANK = 128
CONV_W = 3
NORM_EPS = 1e-6
GN_EPS = 64e-5
NEG_INF = -1e30
EXP_NEG_HALF = math.exp(-0.5)
ATTN_GROUP = 8

assert PAIR == LANES


def _rms_norm(x, g):
    return x * lax.rsqrt(jnp.mean(x * x, axis=-1, keepdims=True) + NORM_EPS) * g


def _resident(shape):
    return pl.BlockSpec(shape, lambda *_: (0,) * len(shape), pipeline_mode=pl.Buffered(1))


def _inproj_kernel(x_ref, g_ref, w_ref, mu_ref, gb_ref, zs_ref, qkv_ref, gate_ref, carry_ref,
                   *, blocks_per_seq, rwkv_cols, attn_cols):
    i = pl.program_id(0)
    rows = x_ref.shape[0]

    @pl.when(i % blocks_per_seq == 0)
    def _():
        carry_ref[...] = jnp.zeros_like(carry_ref)

    hb = _rms_norm(x_ref[...], g_ref[...]).astype(BF16)

    zg = jnp.dot(hb, w_ref[:, rwkv_cols + attn_cols:], preferred_element_type=F32)
    gate_ref[...] = jax.nn.sigmoid(zg + gb_ref[...]).astype(BF16)

    z = jnp.dot(hb, w_ref[:, :rwkv_cols], preferred_element_type=F32)
    row = lax.broadcasted_iota(jnp.int32, z.shape, 0)
    prev = jnp.where(row == 0, carry_ref[0:1, :], pltpu.roll(z, 1, 0))
    zs_ref[...] = z + (prev - z) * mu_ref[...]
    carry_ref[0:1, :] = z[rows - 1:rows, :]

    qkv_ref[...] = jnp.dot(hb, w_ref[:, rwkv_cols:rwkv_cols + attn_cols],
                           preferred_element_type=F32).astype(BF16)


def _inproj(x2, g, w_bf, mu, gate_b, *, seq, rwkv_cols, attn_cols, tm):
    m, d = x2.shape
    gate_cols = w_bf.shape[1] - rwkv_cols - attn_cols
    body = functools.partial(_inproj_kernel, blocks_per_seq=seq // tm,
                             rwkv_cols=rwkv_cols, attn_cols=attn_cols)
    return pl.pallas_call(
        body,
        grid=(m // tm,),
        in_specs=[
            pl.BlockSpec((tm, d), lambda i: (i, 0)),
            _resident((1, d)),
            _resident(w_bf.shape),
            _resident((1, rwkv_cols)),
            _resident((1, gate_cols)),
        ],
        out_specs=[
            pl.BlockSpec((tm, rwkv_cols), lambda i: (i, 0)),
            pl.BlockSpec((tm, attn_cols), lambda i: (i, 0)),
            pl.BlockSpec((tm, gate_cols), lambda i: (i, 0)),
        ],
        out_shape=[
            jax.ShapeDtypeStruct((m, rwkv_cols), F32),
            jax.ShapeDtypeStruct((m, attn_cols), BF16),
            jax.ShapeDtypeStruct((m, gate_cols), BF16),
        ],
        scratch_shapes=[pltpu.VMEM((SUBLANES, rwkv_cols), F32)],
        compiler_params=pltpu.CompilerParams(
            dimension_semantics=("arbitrary",), vmem_limit_bytes=VMEM_LIMIT_BYTES),
        name="inproj",
    )(x2, g, w_bf, mu, gate_b)


def _mm(a, b, dims=None):
    a = a.astype(BF16)
    b = b.astype(BF16)
    if dims is None:
        return jnp.dot(a, b, preferred_element_type=F32)
    return lax.dot_general(a, b, (dims, ((), ())), preferred_element_type=F32)


_NT = ((1,), (1,))
_TN = ((0,), (0,))


def _split2(x):
    hi = x.astype(BF16)
    return hi, (x - hi.astype(F32)).astype(BF16)


def _split3(x):
    hi = x.astype(BF16)
    r1 = x - hi.astype(F32)
    mid = r1.astype(BF16)
    return hi, mid, (r1 - mid.astype(F32)).astype(BF16)


def _unit_lower_inverse_minus_eye(mats, blk16, blk32, same_block):
    n = same_block.shape[0] // CHUNK

    def mul(xs, ys):
        out = []
        for x, y in zip(xs, ys):
            y_diag = jnp.where(same_block, jnp.concatenate([y.astype(BF16)] * n, axis=0),
                               jnp.zeros((), BF16))
            out.append(jnp.dot(x.astype(BF16), y_diag, preferred_element_type=F32))
        return out

    ad = [jnp.where(blk16, a, 0.0) for a in mats]
    a2 = mul(ad, ad)
    a4 = mul(a2, a2)
    p1 = mul(ad, a2)
    a8 = mul(a4, a4)
    p2 = mul(a4, a8)
    x1 = [b - a - p for a, b, p in zip(ad, a2, p1)]
    x2 = [a + b + p for a, b, p in zip(a4, a8, p2)]
    p3 = mul(x1, x2)
    m = [a + b + p for a, b, p in zip(x1, x2, p3)]
    for mask in (jnp.logical_and(blk32, jnp.logical_not(blk16)), jnp.logical_not(blk32)):
        off = [jnp.where(mask, a, 0.0) for a in mats]
        z = [o + p for o, p in zip(off, mul(m, off))]
        m = [mi - zi - p for mi, zi, p in zip(m, z, mul(z, m))]
    return m


def _rwkv_kernel(r_ref, k_ref, v_ref, wa_ref, gd_ref,
                 w0_ref, w2_ref, a0_ref, a2_ref, g2_ref, kk_ref, ka_ref, rk_ref, lnw_ref, lnb_ref,
                 o_ref, state_ref):
    tb, width = r_ref.shape
    n_chunks = tb // CHUNK
    pairs = range(width // PAIR)
    lanes = [slice(q * PAIR, (q + 1) * PAIR) for q in pairs]

    @pl.when(pl.program_id(1) == 0)
    def _():
        state_ref[...] = jnp.zeros_like(state_ref)

    lane = lax.broadcasted_iota(jnp.int32, (1, PAIR), 1)
    head0 = lane < HEAD_DIM
    rr = lax.broadcasted_iota(jnp.int32, (PAIR, PAIR), 0)
    cc = lax.broadcasted_iota(jnp.int32, (PAIR, PAIR), 1)
    same_head = (rr // HEAD_DIM) == (cc // HEAD_DIM)
    ones2 = jnp.concatenate([same_head.astype(BF16)] * 2, axis=0)

    def head_sum(t):
        return jnp.concatenate(
            [jnp.dot(jnp.concatenate(_split2(t[:, l]), axis=1), ones2, preferred_element_type=F32)
             for l in lanes], axis=1)

    r = r_ref[...]
    k = k_ref[...]
    v = v_ref[...]
    wa = wa_ref[...]
    u = w0_ref[...] + _mm(jnp.tanh(wa), w2_ref[...])
    ld = -EXP_NEG_HALF * jax.nn.sigmoid(u)
    a = jax.nn.sigmoid(a0_ref[...] + _mm(wa, a2_ref[...]))
    g = _mm(jax.nn.sigmoid(gd_ref[...]), g2_ref[...])
    kk = k * kk_ref[...]
    kk = kk * lax.rsqrt(jnp.maximum(head_sum(kk * kk), 1e-24))
    kp = k * (1.0 + (a - 1.0) * ka_ref[...])
    bonus = head_sum(r * kp * rk_ref[...]) * v

    wide = 2 * PAIR
    ti = lax.broadcasted_iota(jnp.int32, (CHUNK, wide), 0)
    si = lax.broadcasted_iota(jnp.int32, (CHUNK, wide), 1) % CHUNK
    lower_w = si <= ti
    strict_w = si < ti
    blk16_w = (ti // 16) == (si // 16)
    blk32_w = (ti // 32) == (si // 32)
    bi = lax.broadcasted_iota(jnp.int32, (wide, wide), 0)
    bj = lax.broadcasted_iota(jnp.int32, (wide, wide), 1)
    same_block = (bi // CHUNK) == (bj // CHUNK)
    head0_w = jnp.concatenate([head0, head0], axis=1)
    tri3 = jnp.concatenate([lower_w[:, :CHUNK].astype(BF16)] * 3, axis=1)

    def per_head(t):
        mask = head0 if t.shape[1] == PAIR else head0_w
        return jnp.concatenate([jnp.where(mask, t, 0.0), jnp.where(mask, 0.0, t)], axis=0)

    ld_parts = _split3(ld)
    cum = jnp.concatenate([
        jnp.dot(tri3, jnp.concatenate([part[c * CHUNK:(c + 1) * CHUNK] for part in ld_parts], axis=0),
                preferred_element_type=F32)
        for c in range(n_chunks)], axis=0)
    e_pos = jnp.exp(cum)
    e_neg = jnp.exp(-cum)
    rt_all = r * e_pos
    kt_all = kk * jnp.exp(cum - ld)
    bt_all = kk * a * e_neg
    kpt_all = kp * e_neg

    items = [(q, c) for q in pairs for c in range(n_chunks)]
    n = range(len(items))
    cut = lambda t: [t[c * CHUNK:(c + 1) * CHUNK, lanes[q]] for q, c in items]
    rt, kt, bt, kpt, vc = cut(rt_all), cut(kt_all), cut(bt_all), cut(kpt_all), cut(v)
    gamma_end = [t[CHUNK - 1:CHUNK, :] for t in cut(e_pos)]
    zeros = jnp.zeros((CHUNK, PAIR), F32)

    sc = [_mm(jnp.concatenate([kt[i], rt[i]], axis=0),
              jnp.concatenate([per_head(bt[i]), per_head(kpt[i])], axis=0), _NT) for i in n]
    a_bk = [jnp.where(strict_w, s[:CHUNK], 0.0) for s in sc]
    r_bk = [jnp.where(lower_w, s[CHUNK:], 0.0) for s in sc]
    akv = [_mm(a_bk[i][:, PAIR:], per_head(vc[i])) for i in n]
    t_m = _unit_lower_inverse_minus_eye(
        [jnp.concatenate([a_bk[i][:, :PAIR], a_bk[i + 1][:, :PAIR]], axis=1)
         for i in range(0, len(items), 2)], blk16_w, blk32_w, same_block)
    t_m = [t_m[i // 2][:, (i % 2) * PAIR:(i % 2 + 1) * PAIR] for i in n]
    x = [jnp.concatenate([akv[i], kt[i]], axis=1) for i in n]
    fe = [x[i] + _mm(t_m[i], per_head(x[i])) for i in n]
    f_all = [t[:, :PAIR] for t in fe]
    e_all = [t[:, PAIR:] for t in fe]
    yy = [_mm(r_bk[i], jnp.concatenate([per_head(-fe[i]),
                                        per_head(jnp.concatenate([vc[i], zeros], axis=1))], axis=0))
          for i in n]
    y_own = [t[:, :PAIR] for t in yy]
    ry = [rt[i] + yy[i][:, PAIR:] for i in n]
    b_e = [jnp.where(same_head, _mm(bt[i], e_all[i], _TN), 0.0) for i in n]
    q_g = [jnp.where(same_head, _mm(jnp.concatenate([vc[i], -f_all[i]], axis=0),
                                    jnp.concatenate([kpt[i], bt[i]], axis=0), _TN), 0.0)
           for i in n]

    state = [state_ref[q] for q in pairs]
    y = [[None] * n_chunks for _ in pairs]
    for c in range(n_chunks):
        for q in pairs:
            i = q * n_chunks + c
            y[q][c] = _mm(ry[i], state[q], _NT) + y_own[i]
            state[q] = (state[q] - _mm(state[q], b_e[i], _NT) + q_g[i]) * gamma_end[i]
    for q in pairs:
        state_ref[q] = state[q]

    y = jnp.concatenate([jnp.concatenate(y[q], axis=0) for q in pairs], axis=1)
    mean = head_sum(y) * (1.0 / HEAD_DIM)
    yc = y - mean
    var = head_sum(yc * yc) * (1.0 / HEAD_DIM)
    yn = yc * lax.rsqrt(var + GN_EPS) * lnw_ref[...] + lnb_ref[...]
    o_ref[...] = ((yn + bonus) * g).astype(o_ref.dtype)


def _rwkv(zs3, w0, w2p, a0, a2p, g2, k_k, k_a, r_k, lnx_w, lnx_b, *, width, tb):
    nb, seq, _ = zs3.shape
    assert width % PAIR == 0 and (tb // CHUNK) % 2 == 0
    lora_tile = 3 * width // LANES
    gate_tile = lora_tile + 1

    def tok(cols, col_block):
        return pl.BlockSpec((None, tb, cols), lambda b, t: (b, t, col_block))

    return pl.pallas_call(
        _rwkv_kernel,
        grid=(nb, seq // tb),
        in_specs=[
            tok(width, 0), tok(width, 1), tok(width, 2), tok(LANES, lora_tile), tok(LANES, gate_tile),
            _resident(w0.shape), _resident(w2p.shape), _resident(a0.shape), _resident(a2p.shape),
            _resident(g2.shape), _resident(k_k.shape), _resident(k_a.shape), _resident(r_k.shape),
            _resident(lnx_w.shape), _resident(lnx_b.shape),
        ],
        out_specs=pl.BlockSpec((None, tb, width), lambda b, t: (b, t, 0)),
        out_shape=jax.ShapeDtypeStruct((nb, seq, width), BF16),
        scratch_shapes=[pltpu.VMEM((width // PAIR, PAIR, PAIR), F32)],
        compiler_params=pltpu.CompilerParams(
            dimension_semantics=("parallel", "arbitrary"), vmem_limit_bytes=VMEM_LIMIT_BYTES),
        name="rwkv",
    )(zs3, zs3, zs3, zs3, zs3, w0, w2p, a0, a2p, g2, k_k, k_a, r_k, lnx_w, lnx_b)


def _attn_kernel(q_ref, k_ref, v_ref, bias_ref, o_ref, kpad_ref, vpad_ref, *, scale, group):
    tq = q_ref.shape[0]
    seq = k_ref.shape[0]
    t = pl.program_id(2)

    @pl.when(t == 0)
    def _():
        zeros = jnp.zeros((LEFT_PAD, PAIR), kpad_ref.dtype)
        kpad_ref[0:LEFT_PAD, :] = zeros
        vpad_ref[0:LEFT_PAD, :] = zeros
        kpad_ref[LEFT_PAD:LEFT_PAD + seq, :] = k_ref[...]
        vpad_ref[LEFT_PAD:LEFT_PAD + seq, :] = v_ref[...]

    lane = lax.broadcasted_iota(jnp.int32, (1, PAIR), 1)
    head0 = lane < HEAD_DIM
    head_masks = (head0, jnp.logical_not(head0))
    col = lax.broadcasted_iota(jnp.int32, (CHUNK, BAND), 1)

    def group_body(g, first, masked):
        q, kb, vb = [], [], []
        for j in range(group):
            rows = pl.ds(pl.multiple_of((g * group + j) * CHUNK, CHUNK), CHUNK)
            band = pl.ds(pl.multiple_of((first + j) * CHUNK, CHUNK), BAND)
            q.append(q_ref[rows, :] * scale)
            kb.append(kpad_ref[band, :])
            vb.append(vpad_ref[band, :])
        inst = [(j, h) for j in range(group) for h in range(2)]
        s = [lax.dot_general(jnp.where(head_masks[h], q[j], jnp.zeros_like(q[j])), kb[j],
                             (((1,), (1,)), ((), ())), preferred_element_type=F32) + bias_ref[h]
             for j, h in inst]
        if masked:
            s = [jnp.where(col >= (LEFT_CHUNKS - first - j) * CHUNK, si, NEG_INF)
                 for si, (j, _) in zip(s, inst)]
        p = [jnp.exp(si - jnp.max(si, axis=-1, keepdims=True)) for si in s]
        denom = [jnp.sum(pi, axis=-1, keepdims=True) for pi in p]
        o = [jnp.dot(pi.astype(vb[j].dtype), vb[j], preferred_element_type=F32) / di
             for pi, di, (j, _) in zip(p, denom, inst)]
        for j in range(group):
            rows = pl.ds(pl.multiple_of((g * group + j) * CHUNK, CHUNK), CHUNK)
            o_ref[rows, :] = jnp.where(head0, o[2 * j], o[2 * j + 1]).astype(o_ref.dtype)

    def group_step(g, carry):
        first = t * (tq // CHUNK) + g * group
        lax.cond(first < LEFT_CHUNKS,
                 lambda: group_body(g, first, True), lambda: group_body(g, first, False))
        return carry

    lax.fori_loop(0, tq // (CHUNK * group), group_step, 0)


def _attention(qkv3, bias, *, width, tq):
    nb, seq, _ = qkv3.shape
    n_pairs = width // PAIR
    scale = HEAD_DIM ** -0.5
    assert math.frexp(scale)[0] == 0.5, "folding the scale into bf16 q is exact only for a power of two"
    return pl.pallas_call(
        functools.partial(_attn_kernel, scale=scale, group=ATTN_GROUP),
        grid=(nb, n_pairs, seq // tq),
        in_specs=[
            pl.BlockSpec((None, tq, PAIR), lambda b, p, t: (b, t, p)),
            pl.BlockSpec((None, seq, PAIR), lambda b, p, t: (b, 0, n_pairs + p)),
            pl.BlockSpec((None, seq, PAIR), lambda b, p, t: (b, 0, 2 * n_pairs + p)),
            pl.BlockSpec((2, CHUNK, BAND), lambda b, p, t: (p, 0, 0)),
        ],
        out_specs=pl.BlockSpec((None, tq, PAIR), lambda b, p, t: (b, t, p)),
        out_shape=jax.ShapeDtypeStruct((nb, seq, width), BF16),
        scratch_shapes=[pltpu.VMEM((LEFT_PAD + seq, PAIR), BF16)] * 2,
        compiler_params=pltpu.CompilerParams(
            dimension_semantics=("parallel", "parallel", "arbitrary"),
            vmem_limit_bytes=VMEM_LIMIT_BYTES),
        name="chunk_attention",
    )(qkv3, qkv3, qkv3, bias)


def _band_bias(rel_bias):
    period = BAND + CHUNK
    n = jnp.arange(period)
    j_minus_q = jnp.where(n < BAND, n, n - period)
    idx = jnp.clip(LEFT_PAD - j_minus_q, -(CHUNK - 1), MAX_REL) + (CHUNK - 1)
    diagonals = rel_bias.astype(F32)[:, idx]
    skewed = jnp.tile(diagonals, (1, CHUNK))[:, :CHUNK * (period - 1)]
    return skewed.reshape(-1, CHUNK, period - 1)[:, :, :BAND]


def _ff_tiles(d_ff):
    passes = d_ff // MXU_WIDTH
    if d_ff % MXU_WIDTH or passes < 2:
        return [(0, d_ff)]
    split = (passes + 1) // 2 * MXU_WIDTH
    return [(0, split), (split, d_ff)]


def _mix_kernel(x_ref, oa_ref, ob_ref, gate_ref, p_ref, wa_ref, wb_ref, wo_ref, g2_ref, wup_ref,
                cw_ref, cb_ref, wdn_ref, g3_ref, wple_ref, wpg_ref, bpg_ref, gf_ref, o_ref, carry_ref,
                *, blocks_per_seq, final_norm):
    i = pl.program_id(0)
    rows, d = x_ref.shape
    d_ff = wdn_ref.shape[0]

    @pl.when(i % blocks_per_seq == 0)
    def _():
        carry_ref[...] = jnp.zeros_like(carry_ref)

    br_a = jnp.dot(oa_ref[...], wa_ref[...], preferred_element_type=F32)
    br_b = jnp.dot(ob_ref[...], wb_ref[...], preferred_element_type=F32)
    merged = gate_ref[:, :d].astype(F32) * br_a + gate_ref[:, d:].astype(F32) * br_b
    x = x_ref[...] + jnp.dot(merged.astype(BF16), wo_ref[...], preferred_element_type=F32)

    hb = _rms_norm(x, g2_ref[...]).astype(BF16)
    acc = x
    for lo, hi in _ff_tiles(d_ff):
        cols = slice(lo, hi)
        row = lax.broadcasted_iota(jnp.int32, (rows, hi - lo), 0)
        a = jnp.dot(hb, wup_ref[:, cols], preferred_element_type=F32)
        gv = jnp.dot(hb, wup_ref[:, d_ff + lo:d_ff + hi], preferred_element_type=F32)
        c0 = carry_ref[0:1, cols]
        c1 = carry_ref[1:2, cols]
        prev1 = jnp.where(row == 0, c1, pltpu.roll(a, 1, 0))
        prev2 = jnp.where(row == 0, c0, jnp.where(row == 1, c1, pltpu.roll(a, 2, 0)))
        carry_ref[0:2, cols] = a[rows - 2:rows, :]
        conv = (cw_ref[0:1, cols] * prev2 + cw_ref[1:2, cols] * prev1 + cw_ref[2:3, cols] * a
                + cb_ref[:, cols])
        act = 0.5 * conv * (1.0 + lax.erf(conv * math.sqrt(0.5))) * gv
        acc = acc + jnp.dot(act.astype(BF16), wdn_ref[cols, :], preferred_element_type=F32)

    h3 = _rms_norm(acc, g3_ref[...]).astype(BF16)
    pg = jax.nn.sigmoid(jnp.dot(h3, wpg_ref[...], preferred_element_type=F32) + bpg_ref[...])
    ple = jnp.dot(p_ref[...].astype(BF16), wple_ref[...], preferred_element_type=F32)
    out = acc + ple * pg
    if final_norm:
        out = _rms_norm(out, gf_ref[...])
    o_ref[...] = out


def _mix(x2, oa, ob, gates, p2, w_a, w_b, w_o, g2, w_up, conv_w, conv_b, w_down, g3, w_ple, w_pg,
         b_pg, gf, *, seq, tm, final_norm):
    m, d = x2.shape
    d_ff = w_down.shape[0]
    body = functools.partial(_mix_kernel, blocks_per_seq=seq // tm, final_norm=final_norm)
    row = lambda t: pl.BlockSpec((tm, t.shape[1]), lambda i: (i, 0))
    weights = (w_a, w_b, w_o, g2, w_up, conv_w, conv_b, w_down, g3, w_ple, w_pg, b_pg, gf)
    return pl.pallas_call(
        body,
        grid=(m // tm,),
        in_specs=[row(x2), row(oa), row(ob), row(gates), row(p2)]
                 + [_resident(w.shape) for w in weights],
        out_specs=row(x2),
        out_shape=jax.ShapeDtypeStruct((m, d), F32),
        scratch_shapes=[pltpu.VMEM((SUBLANES, d_ff), F32)],
        compiler_params=pltpu.CompilerParams(
            dimension_semantics=("arbitrary",), vmem_limit_bytes=VMEM_LIMIT_BYTES),
        name="mix",
    )(x2, oa, ob, gates, p2, *weights)


def kernel(x, p, ln1_g, w_in, mix_mu, w0, w2, a0, a2, g2, k_k, k_a, r_k, lnx_w, lnx_b, rel_bias, gate_b, w_br_rwkv, w_br_attn, w_o, ln2_g, w_ffn_up, conv_w, conv_b, w_ffn_down, ln3_g, w_ple, w_pg, b_pg, lnf_g):
    nb, seq, d = x.shape
    depth = w_in.shape[0]
    rwkv_width = w_br_rwkv.shape[1]
    attn_width = w_br_attn.shape[1]
    rwkv_cols = 3 * rwkv_width + DECAY_RANK + ICLR_RANK + GATE_RANK
    attn_cols = 3 * attn_width
    assert DECAY_RANK + ICLR_RANK == LANES and GATE_RANK == LANES
    assert seq % 512 == 0 and seq >= BAND
    row = lambda t: t.reshape(1, -1)
    m = nb * seq
    x2 = x.reshape(m, d)

    for i in range(depth):
        last = i == depth - 1
        zs, qkv, gates = _inproj(
            x2, row(ln1_g[i]), w_in[i].astype(BF16), row(mix_mu[i]), row(gate_b[i]),
            seq=seq, rwkv_cols=rwkv_cols, attn_cols=attn_cols, tm=512)

        w2p = jnp.concatenate([w2[i], jnp.zeros_like(a2[i])], axis=0)
        a2p = jnp.concatenate([jnp.zeros_like(w2[i]), a2[i]], axis=0)
        o_a = _rwkv(zs.reshape(nb, seq, rwkv_cols), row(w0[i]), w2p, row(a0[i]), a2p, g2[i],
                    row(k_k[i]), row(k_a[i]), row(r_k[i]), row(lnx_w[i]), row(lnx_b[i]),
                    width=rwkv_width, tb=512)

        o_b = _attention(qkv.reshape(nb, seq, attn_cols), _band_bias(rel_bias[i]),
                         width=attn_width, tq=1024)

        x2 = _mix(x2, o_a.reshape(m, rwkv_width), o_b.reshape(m, attn_width), gates,
                  p[i].reshape(m, -1), w_br_rwkv[i].astype(BF16), w_br_attn[i].astype(BF16),
                  w_o[i].astype(BF16), row(ln2_g[i]), w_ffn_up[i].astype(BF16), conv_w[i],
                  row(conv_b[i]), w_ffn_down[i].astype(BF16), row(ln3_g[i]), w_ple[i].astype(BF16),
                  w_pg[i].astype(BF16), row(b_pg[i]), row(lnf_g), seq=seq, tm=512, final_norm=last)
    return x2.reshape(nb, seq, d)
```

```python
import functools
import math

import jax
import jax.numpy as jnp
from jax import lax
from jax.experimental import pallas as pl
from jax.experimental.pallas import tpu as pltpu

F32 = jnp.float32
BF16 = jnp.bfloat16

LANES = 128
SUBLANES = 8
MXU_WIDTH = 256
VMEM_LIMIT_BYTES = 56 * 1024 * 1024

HEAD_DIM = 64
PAIR = 2 * HEAD_DIM
CHUNK = 64
LEFT_CHUNKS = 8
BAND = (LEFT_CHUNKS + 1) * CHUNK
LEFT_PAD = LEFT_CHUNKS * CHUNK
MAX_REL = 128
DECAY_RANK = 64
ICLR_RANK = 64
GATE_RANK = 128
CONV_W = 3
NORM_EPS = 1e-6
GN_EPS = 64e-5
NEG_INF = -1e30
EXP_NEG_HALF = math.exp(-0.5)
ATTN_GROUP = 8

assert PAIR == LANES


def _rms_norm(x, g):
    return x * lax.rsqrt(jnp.mean(x * x, axis=-1, keepdims=True) + NORM_EPS) * g


def _resident(shape):
    return pl.BlockSpec(shape, lambda *_: (0,) * len(shape), pipeline_mode=pl.Buffered(1))


def _inproj_kernel(x_ref, g_ref, w_ref, mu_ref, gb_ref, zs_ref, qkv_ref, gate_ref, carry_ref,
                   *, blocks_per_seq, rwkv_cols, attn_cols):
    i = pl.program_id(0)
    rows = x_ref.shape[0]

    @pl.when(i % blocks_per_seq == 0)
    def _():
        carry_ref[...] = jnp.zeros_like(carry_ref)

    hb = _rms_norm(x_ref[...], g_ref[...]).astype(BF16)

    zg = jnp.dot(hb, w_ref[:, rwkv_cols + attn_cols:], preferred_element_type=F32)
    gate_ref[...] = jax.nn.sigmoid(zg + gb_ref[...]).astype(BF16)

    z = jnp.dot(hb, w_ref[:, :rwkv_cols], preferred_element_type=F32)
    row = lax.broadcasted_iota(jnp.int32, z.shape, 0)
    prev = jnp.where(row == 0, carry_ref[0:1, :], pltpu.roll(z, 1, 0))
    zs_ref[...] = z + (prev - z) * mu_ref[...]
    carry_ref[0:1, :] = z[rows - 1:rows, :]

    qkv_ref[...] = jnp.dot(hb, w_ref[:, rwkv_cols:rwkv_cols + attn_cols],
                           preferred_element_type=F32).astype(BF16)


def _inproj(x2, g, w_bf, mu, gate_b, *, seq, rwkv_cols, attn_cols, tm):
    m, d = x2.shape
    gate_cols = w_bf.shape[1] - rwkv_cols - attn_cols
    body = functools.partial(_inproj_kernel, blocks_per_seq=seq // tm,
                             rwkv_cols=rwkv_cols, attn_cols=attn_cols)
    return pl.pallas_call(
        body,
        grid=(m // tm,),
        in_specs=[
            pl.BlockSpec((tm, d), lambda i: (i, 0)),
            _resident((1, d)),
            _resident(w_bf.shape),
            _resident((1, rwkv_cols)),
            _resident((1, gate_cols)),
        ],
        out_specs=[
            pl.BlockSpec((tm, rwkv_cols), lambda i: (i, 0)),
            pl.BlockSpec((tm, attn_cols), lambda i: (i, 0)),
            pl.BlockSpec((tm, gate_cols), lambda i: (i, 0)),
        ],
        out_shape=[
            jax.ShapeDtypeStruct((m, rwkv_cols), F32),
            jax.ShapeDtypeStruct((m, attn_cols), BF16),
            jax.ShapeDtypeStruct((m, gate_cols), BF16),
        ],
        scratch_shapes=[pltpu.VMEM((SUBLANES, rwkv_cols), F32)],
        compiler_params=pltpu.CompilerParams(
            dimension_semantics=("arbitrary",), vmem_limit_bytes=VMEM_LIMIT_BYTES),
        name="inproj",
    )(x2, g, w_bf, mu, gate_b)


def _mm(a, b, dims=None):
    a = a.astype(BF16)
    b = b.astype(BF16)
    if dims is None:
        return jnp.dot(a, b, preferred_element_type=F32)
    return lax.dot_general(a, b, (dims, ((), ())), preferred_element_type=F32)


_NT = ((1,), (1,))
_TN = ((0,), (0,))


def _split2(x):
    hi = x.astype(BF16)
    return hi, (x - hi.astype(F32)).astype(BF16)


def _mm_split(a, b):
    rows = a.shape[0]
    a_hi, a_lo = _split2(a)
    b_hi, b_lo = _split2(b)
    both = jnp.dot(jnp.concatenate([a_hi, a_lo], axis=0), b_hi, preferred_element_type=F32)
    return both[:rows] + both[rows:] + jnp.dot(a_hi, b_lo, preferred_element_type=F32)


def _split3(x):
    hi = x.astype(BF16)
    r1 = x - hi.astype(F32)
    mid = r1.astype(BF16)
    return hi, mid, (r1 - mid.astype(F32)).astype(BF16)


def _unit_lower_inverse_minus_eye(mats, blk16, blk32, same_block):
    n = same_block.shape[0] // CHUNK

    def diag(t):
        return jnp.where(same_block, jnp.concatenate([t] * n, axis=0), jnp.zeros((), BF16))

    def mul(xs, ys):
        return [jnp.dot(x.astype(BF16), diag(y.astype(BF16)), preferred_element_type=F32)
                for x, y in zip(xs, ys)]

    def split(ts):
        parts = [_split2(t) for t in ts]
        return [(hi, lo, diag(hi), diag(lo)) for hi, lo in parts]

    def mul_split(xs, ys):
        out = []
        for (x_hi, x_lo, _, _), (_, _, y_hi, y_lo) in zip(xs, ys):
            both = jnp.dot(jnp.concatenate([x_hi, x_lo], axis=0), y_hi, preferred_element_type=F32)
            out.append(both[:CHUNK] + both[CHUNK:]
                       + jnp.dot(x_hi, y_lo, preferred_element_type=F32))
        return out

    ad = [jnp.where(blk16, a, 0.0) for a in mats]
    ad_s = split(ad)
    a2 = mul_split(ad_s, ad_s)
    a2_s = split(a2)
    a4 = mul_split(a2_s, a2_s)
    p1 = mul_split(ad_s, a2_s)
    a4_s = split(a4)
    a8 = mul_split(a4_s, a4_s)
    p2 = mul_split(a4_s, split(a8))
    x1 = [b - a - p for a, b, p in zip(ad, a2, p1)]
    x2 = [a + b + p for a, b, p in zip(a4, a8, p2)]
    p3 = mul_split(split(x1), split(x2))
    m = [a + b + p for a, b, p in zip(x1, x2, p3)]
    for mask in (jnp.logical_and(blk32, jnp.logical_not(blk16)), jnp.logical_not(blk32)):
        off = [jnp.where(mask, a, 0.0) for a in mats]
        z = [o + p for o, p in zip(off, mul(m, off))]
        m = [mi - zi - p for mi, zi, p in zip(m, z, mul(z, m))]
    return m


def _rwkv_kernel(r_ref, k_ref, v_ref, wa_ref, gd_ref,
                 w0_ref, w2_ref, a0_ref, a2_ref, g2_ref, kk_ref, ka_ref, rk_ref, lnw_ref, lnb_ref,
                 o_ref, state_ref):
    tb, width = r_ref.shape
    n_chunks = tb // CHUNK
    pairs = range(width // PAIR)
    lanes = [slice(q * PAIR, (q + 1) * PAIR) for q in pairs]

    @pl.when(pl.program_id(1) == 0)
    def _():
        state_ref[...] = jnp.zeros_like(state_ref)

    lane = lax.broadcasted_iota(jnp.int32, (1, PAIR), 1)
    head0 = lane < HEAD_DIM
    rr = lax.broadcasted_iota(jnp.int32, (PAIR, PAIR), 0)
    cc = lax.broadcasted_iota(jnp.int32, (PAIR, PAIR), 1)
    same_head = (rr // HEAD_DIM) == (cc // HEAD_DIM)
    ones2 = jnp.concatenate([same_head.astype(BF16)] * 2, axis=0)

    def head_sum(t):
        return jnp.concatenate(
            [jnp.dot(jnp.concatenate(_split2(t[:, l]), axis=1), ones2, preferred_element_type=F32)
             for l in lanes], axis=1)

    r = r_ref[...]
    k = k_ref[...]
    v = v_ref[...]
    wa = wa_ref[...]
    u = w0_ref[...] + _mm(jnp.tanh(wa), w2_ref[...])
    ld = -EXP_NEG_HALF * jax.nn.sigmoid(u)
    a = jax.nn.sigmoid(a0_ref[...] + _mm(wa, a2_ref[...]))
    g = _mm(jax.nn.sigmoid(gd_ref[...]), g2_ref[...])
    kk = k * kk_ref[...]
    kk = kk * lax.rsqrt(jnp.maximum(head_sum(kk * kk), 1e-24))
    kp = k * (1.0 + (a - 1.0) * ka_ref[...])
    bonus = head_sum(r * kp * rk_ref[...]) * v

    wide = 2 * PAIR
    ti = lax.broadcasted_iota(jnp.int32, (CHUNK, wide), 0)
    si = lax.broadcasted_iota(jnp.int32, (CHUNK, wide), 1) % CHUNK
    lower_w = si <= ti
    strict_w = si < ti
    blk16_w = (ti // 16) == (si // 16)
    blk32_w = (ti // 32) == (si // 32)
    bi = lax.broadcasted_iota(jnp.int32, (wide, wide), 0)
    bj = lax.broadcasted_iota(jnp.int32, (wide, wide), 1)
    same_block = (bi // CHUNK) == (bj // CHUNK)
    head0_w = jnp.concatenate([head0, head0], axis=1)
    tri3 = jnp.concatenate([lower_w[:, :CHUNK].astype(BF16)] * 3, axis=1)

    def per_head(t):
        mask = head0 if t.shape[1] == PAIR else head0_w
        return jnp.concatenate([jnp.where(mask, t, 0.0), jnp.where(mask, 0.0, t)], axis=0)

    ld_parts = _split3(ld)
    cum = jnp.concatenate([
        jnp.dot(tri3, jnp.concatenate([part[c * CHUNK:(c + 1) * CHUNK] for part in ld_parts], axis=0),
                preferred_element_type=F32)
        for c in range(n_chunks)], axis=0)
    e_pos = jnp.exp(cum)
    e_neg = jnp.exp(-cum)
    rt_all = r * e_pos
    kt_all = kk * jnp.exp(cum - ld)
    bt_all = kk * a * e_neg
    kpt_all = kp * e_neg

    items = [(q, c) for q in pairs for c in range(n_chunks)]
    n = range(len(items))
    cut = lambda t: [t[c * CHUNK:(c + 1) * CHUNK, lanes[q]] for q, c in items]
    rt, kt, bt, kpt, vc = cut(rt_all), cut(kt_all), cut(bt_all), cut(kpt_all), cut(v)
    gamma_end = [t[CHUNK - 1:CHUNK, :] for t in cut(e_pos)]
    zeros = jnp.zeros((CHUNK, PAIR), F32)

    sc = [_mm(jnp.concatenate([kt[i], rt[i]], axis=0),
              jnp.concatenate([per_head(bt[i]), per_head(kpt[i])], axis=0), _NT) for i in n]
    a_bk = [jnp.where(strict_w, s[:CHUNK], 0.0) for s in sc]
    r_bk = [jnp.where(lower_w, s[CHUNK:], 0.0) for s in sc]
    akv = [_mm(a_bk[i][:, PAIR:], per_head(vc[i])) for i in n]
    t_m = _unit_lower_inverse_minus_eye(
        [jnp.concatenate([a_bk[i][:, :PAIR], a_bk[i + 1][:, :PAIR]], axis=1)
         for i in range(0, len(items), 2)], blk16_w, blk32_w, same_block)
    t_m = [t_m[i // 2][:, (i % 2) * PAIR:(i % 2 + 1) * PAIR] for i in n]
    x = [jnp.concatenate([akv[i], kt[i]], axis=1) for i in n]
    fe = [x[i] + _mm_split(t_m[i], per_head(x[i])) for i in n]
    f_all = [t[:, :PAIR] for t in fe]
    e_all = [t[:, PAIR:] for t in fe]
    yy = [_mm(r_bk[i], jnp.concatenate([per_head(-fe[i]),
                                        per_head(jnp.concatenate([vc[i], zeros], axis=1))], axis=0))
          for i in n]
    y_own = [t[:, :PAIR] for t in yy]
    ry = [rt[i] + yy[i][:, PAIR:] for i in n]
    b_e = [jnp.where(same_head, _mm(bt[i], e_all[i], _TN), 0.0) for i in n]
    q_g = [jnp.where(same_head, _mm(jnp.concatenate([vc[i], -f_all[i]], axis=0),
                                    jnp.concatenate([kpt[i], bt[i]], axis=0), _TN), 0.0)
           for i in n]

    state = [state_ref[q] for q in pairs]
    y = [[None] * n_chunks for _ in pairs]
    for c in range(n_chunks):
        for q in pairs:
            i = q * n_chunks + c
            y[q][c] = _mm(ry[i], state[q], _NT) + y_own[i]
            state[q] = (state[q] - _mm(state[q], b_e[i], _NT) + q_g[i]) * gamma_end[i]
    for q in pairs:
        state_ref[q] = state[q]

    y = jnp.concatenate([jnp.concatenate(y[q], axis=0) for q in pairs], axis=1)
    mean = head_sum(y) * (1.0 / HEAD_DIM)
    yc = y - mean
    var = head_sum(yc * yc) * (1.0 / HEAD_DIM)
    yn = yc * lax.rsqrt(var + GN_EPS) * lnw_ref[...] + lnb_ref[...]
    o_ref[...] = ((yn + bonus) * g).astype(o_ref.dtype)


def _rwkv(zs3, w0, w2p, a0, a2p, g2, k_k, k_a, r_k, lnx_w, lnx_b, *, width, tb):
    nb, seq, _ = zs3.shape
    assert width % PAIR == 0 and (tb // CHUNK) % 2 == 0
    lora_tile = 3 * width // LANES
    gate_tile = lora_tile + 1

    def tok(cols, col_block):
        return pl.BlockSpec((None, tb, cols), lambda b, t: (b, t, col_block))

    return pl.pallas_call(
        _rwkv_kernel,
        grid=(nb, seq // tb),
        in_specs=[
            tok(width, 0), tok(width, 1), tok(width, 2), tok(LANES, lora_tile), tok(LANES, gate_tile),
            _resident(w0.shape), _resident(w2p.shape), _resident(a0.shape), _resident(a2p.shape),
            _resident(g2.shape), _resident(k_k.shape), _resident(k_a.shape), _resident(r_k.shape),
            _resident(lnx_w.shape), _resident(lnx_b.shape),
        ],
        out_specs=pl.BlockSpec((None, tb, width), lambda b, t: (b, t, 0)),
        out_shape=jax.ShapeDtypeStruct((nb, seq, width), BF16),
        scratch_shapes=[pltpu.VMEM((width // PAIR, PAIR, PAIR), F32)],
        compiler_params=pltpu.CompilerParams(
            dimension_semantics=("parallel", "arbitrary"), vmem_limit_bytes=VMEM_LIMIT_BYTES),
        name="rwkv",
    )(zs3, zs3, zs3, zs3, zs3, w0, w2p, a0, a2p, g2, k_k, k_a, r_k, lnx_w, lnx_b)


def _attn_kernel(q_ref, k_ref, v_ref, bias_ref, o_ref, kpad_ref, vpad_ref, *, scale, group):
    tq = q_ref.shape[0]
    seq = k_ref.shape[0]
    t = pl.program_id(2)

    @pl.when(t == 0)
    def _():
        zeros = jnp.zeros((LEFT_PAD, PAIR), kpad_ref.dtype)
        kpad_ref[0:LEFT_PAD, :] = zeros
        vpad_ref[0:LEFT_PAD, :] = zeros
        kpad_ref[LEFT_PAD:LEFT_PAD + seq, :] = k_ref[...]
        vpad_ref[LEFT_PAD:LEFT_PAD + seq, :] = v_ref[...]

    lane = lax.broadcasted_iota(jnp.int32, (1, PAIR), 1)
    head0 = lane < HEAD_DIM
    head_masks = (head0, jnp.logical_not(head0))
    col = lax.broadcasted_iota(jnp.int32, (CHUNK, BAND), 1)

    def group_body(g, first, masked):
        q, kb, vb = [], [], []
        for j in range(group):
            rows = pl.ds(pl.multiple_of((g * group + j) * CHUNK, CHUNK), CHUNK)
            band = pl.ds(pl.multiple_of((first + j) * CHUNK, CHUNK), BAND)
            q.append(q_ref[rows, :] * scale)
            kb.append(kpad_ref[band, :])
            vb.append(vpad_ref[band, :])
        inst = [(j, h) for j in range(group) for h in range(2)]
        s = [lax.dot_general(jnp.where(head_masks[h], q[j], jnp.zeros_like(q[j])), kb[j],
                             (((1,), (1,)), ((), ())), preferred_element_type=F32) + bias_ref[h]
             for j, h in inst]
        if masked:
            s = [jnp.where(col >= (LEFT_CHUNKS - first - j) * CHUNK, si, NEG_INF)
                 for si, (j, _) in zip(s, inst)]
        p = [jnp.exp(si - jnp.max(si, axis=-1, keepdims=True)) for si in s]
        denom = [jnp.sum(pi, axis=-1, keepdims=True) for pi in p]
        o = [jnp.dot(pi.astype(vb[j].dtype), vb[j], preferred_element_type=F32) / di
             for pi, di, (j, _) in zip(p, denom, inst)]
        for j in range(group):
            rows = pl.ds(pl.multiple_of((g * group + j) * CHUNK, CHUNK), CHUNK)
            o_ref[rows, :] = jnp.where(head0, o[2 * j], o[2 * j + 1]).astype(o_ref.dtype)

    def group_step(g, carry):
        first = t * (tq // CHUNK) + g * group
        lax.cond(first < LEFT_CHUNKS,
                 lambda: group_body(g, first, True), lambda: group_body(g, first, False))
        return carry

    lax.fori_loop(0, tq // (CHUNK * group), group_step, 0)


def _attention(qkv3, bias, *, width, tq):
    nb, seq, _ = qkv3.shape
    n_pairs = width // PAIR
    scale = HEAD_DIM ** -0.5
    assert math.frexp(scale)[0] == 0.5, "folding the scale into bf16 q is exact only for a power of two"
    return pl.pallas_call(
        functools.partial(_attn_kernel, scale=scale, group=ATTN_GROUP),
        grid=(nb, n_pairs, seq // tq),
        in_specs=[
            pl.BlockSpec((None, tq, PAIR), lambda b, p, t: (b, t, p)),
            pl.BlockSpec((None, seq, PAIR), lambda b, p, t: (b, 0, n_pairs + p)),
            pl.BlockSpec((None, seq, PAIR), lambda b, p, t: (b, 0, 2 * n_pairs + p)),
            pl.BlockSpec((2, CHUNK, BAND), lambda b, p, t: (p, 0, 0)),
        ],
        out_specs=pl.BlockSpec((None, tq, PAIR), lambda b, p, t: (b, t, p)),
        out_shape=jax.ShapeDtypeStruct((nb, seq, width), BF16),
        scratch_shapes=[pltpu.VMEM((LEFT_PAD + seq, PAIR), BF16)] * 2,
        compiler_params=pltpu.CompilerParams(
            dimension_semantics=("parallel", "parallel", "arbitrary"),
            vmem_limit_bytes=VMEM_LIMIT_BYTES),
        name="chunk_attention",
    )(qkv3, qkv3, qkv3, bias)


def _band_bias(rel_bias):
    period = BAND + CHUNK
    n = jnp.arange(period)
    j_minus_q = jnp.where(n < BAND, n, n - period)
    idx = jnp.clip(LEFT_PAD - j_minus_q, -(CHUNK - 1), MAX_REL) + (CHUNK - 1)
    diagonals = rel_bias.astype(F32)[:, idx]
    skewed = jnp.tile(diagonals, (1, CHUNK))[:, :CHUNK * (period - 1)]
    return skewed.reshape(-1, CHUNK, period - 1)[:, :, :BAND]


def _ff_tiles(d_ff):
    passes = d_ff // MXU_WIDTH
    if d_ff % MXU_WIDTH or passes < 2:
        return [(0, d_ff)]
    split = (passes + 1) // 2 * MXU_WIDTH
    return [(0, split), (split, d_ff)]


def _mix_kernel(x_ref, oa_ref, ob_ref, gate_ref, p_ref, wa_ref, wb_ref, wo_ref, g2_ref, wup_ref,
                cw_ref, cb_ref, wdn_ref, g3_ref, wple_ref, wpg_ref, bpg_ref, gf_ref, o_ref, carry_ref,
                *, blocks_per_seq, final_norm):
    i = pl.program_id(0)
    rows, d = x_ref.shape
    d_ff = wdn_ref.shape[0]

    @pl.when(i % blocks_per_seq == 0)
    def _():
        carry_ref[...] = jnp.zeros_like(carry_ref)

    br_a = jnp.dot(oa_ref[...], wa_ref[...], preferred_element_type=F32)
    br_b = jnp.dot(ob_ref[...], wb_ref[...], preferred_element_type=F32)
    merged = gate_ref[:, :d].astype(F32) * br_a + gate_ref[:, d:].astype(F32) * br_b
    x = x_ref[...] + jnp.dot(merged.astype(BF16), wo_ref[...], preferred_element_type=F32)

    hb = _rms_norm(x, g2_ref[...]).astype(BF16)
    acc = x
    for lo, hi in _ff_tiles(d_ff):
        cols = slice(lo, hi)
        row = lax.broadcasted_iota(jnp.int32, (rows, hi - lo), 0)
        a = jnp.dot(hb, wup_ref[:, cols], preferred_element_type=F32)
        gv = jnp.dot(hb, wup_ref[:, d_ff + lo:d_ff + hi], preferred_element_type=F32)
        c0 = carry_ref[0:1, cols]
        c1 = carry_ref[1:2, cols]
        prev1 = jnp.where(row == 0, c1, pltpu.roll(a, 1, 0))
        prev2 = jnp.where(row == 0, c0, jnp.where(row == 1, c1, pltpu.roll(a, 2, 0)))
        carry_ref[0:2, cols] = a[rows - 2:rows, :]
        conv = (cw_ref[0:1, cols] * prev2 + cw_ref[1:2, cols] * prev1 + cw_ref[2:3, cols] * a
                + cb_ref[:, cols])
        act = 0.5 * conv * (1.0 + lax.erf(conv * math.sqrt(0.5))) * gv
        acc = acc + jnp.dot(act.astype(BF16), wdn_ref[cols, :], preferred_element_type=F32)

    h3 = _rms_norm(acc, g3_ref[...]).astype(BF16)
    pg = jax.nn.sigmoid(jnp.dot(h3, wpg_ref[...], preferred_element_type=F32) + bpg_ref[...])
    ple = jnp.dot(p_ref[...].astype(BF16), wple_ref[...], preferred_element_type=F32)
    out = acc + ple * pg
    if final_norm:
        out = _rms_norm(out, gf_ref[...])
    o_ref[...] = out


def _mix(x2, oa, ob, gates, p2, w_a, w_b, w_o, g2, w_up, conv_w, conv_b, w_down, g3, w_ple, w_pg,
         b_pg, gf, *, seq, tm, final_norm):
    m, d = x2.shape
    d_ff = w_down.shape[0]
    body = functools.partial(_mix_kernel, blocks_per_seq=seq // tm, final_norm=final_norm)
    row = lambda t: pl.BlockSpec((tm, t.shape[1]), lambda i: (i, 0))
    weights = (w_a, w_b, w_o, g2, w_up, conv_w, conv_b, w_down, g3, w_ple, w_pg, b_pg, gf)
    return pl.pallas_call(
        body,
        grid=(m // tm,),
        in_specs=[row(x2), row(oa), row(ob), row(gates), row(p2)]
                 + [_resident(w.shape) for w in weights],
        out_specs=row(x2),
        out_shape=jax.ShapeDtypeStruct((m, d), F32),
        scratch_shapes=[pltpu.VMEM((SUBLANES, d_ff), F32)],
        compiler_params=pltpu.CompilerParams(
            dimension_semantics=("arbitrary",), vmem_limit_bytes=VMEM_LIMIT_BYTES),
        name="mix",
    )(x2, oa, ob, gates, p2, *weights)


def kernel(x, p, ln1_g, w_in, mix_mu, w0, w2, a0, a2, g2, k_k, k_a, r_k, lnx_w, lnx_b, rel_bias, gate_b, w_br_rwkv, w_br_attn, w_o, ln2_g, w_ffn_up, conv_w, conv_b, w_ffn_down, ln3_g, w_ple, w_pg, b_pg, lnf_g):
    nb, seq, d = x.shape
    depth = w_in.shape[0]
    rwkv_width = w_br_rwkv.shape[1]
    attn_width = w_br_attn.shape[1]
    rwkv_cols = 3 * rwkv_width + DECAY_RANK + ICLR_RANK + GATE_RANK
    attn_cols = 3 * attn_width
    assert DECAY_RANK + ICLR_RANK == LANES and GATE_RANK == LANES
    assert seq % 512 == 0 and seq >= BAND
    row = lambda t: t.reshape(1, -1)
    m = nb * seq
    x2 = x.reshape(m, d)

    for i in range(depth):
        last = i == depth - 1
        zs, qkv, gates = _inproj(
            x2, row(ln1_g[i]), w_in[i].astype(BF16), row(mix_mu[i]), row(gate_b[i]),
            seq=seq, rwkv_cols=rwkv_cols, attn_cols=attn_cols, tm=512)

        w2p = jnp.concatenate([w2[i], jnp.zeros_like(a2[i])], axis=0)
        a2p = jnp.concatenate([jnp.zeros_like(w2[i]), a2[i]], axis=0)
        o_a = _rwkv(zs.reshape(nb, seq, rwkv_cols), row(w0[i]), w2p, row(a0[i]), a2p, g2[i],
                    row(k_k[i]), row(k_a[i]), row(r_k[i]), row(lnx_w[i]), row(lnx_b[i]),
                    width=rwkv_width, tb=512)

        o_b = _attention(qkv.reshape(nb, seq, attn_cols), _band_bias(rel_bias[i]),
                         width=attn_width, tq=1024)

        x2 = _mix(x2, o_a.reshape(m, rwkv_width), o_b.reshape(m, attn_width), gates,
                  p[i].reshape(m, -1), w_br_rwkv[i].astype(BF16), w_br_attn[i].astype(BF16),
                  w_o[i].astype(BF16), row(ln2_g[i]), w_ffn_up[i].astype(BF16), conv_w[i],
                  row(conv_b[i]), w_ffn_down[i].astype(BF16), row(ln3_g[i]), w_ple[i].astype(BF16),
                  w_pg[i].astype(BF16), row(b_pg[i]), row(lnf_g), seq=seq, tm=512, final_norm=last)
    return x2.reshape(nb, seq, d)
```

```python
import functools
import math

import jax
import jax.numpy as jnp
from jax import lax
from jax.experimental import pallas as pl
from jax.experimental.pallas import tpu as pltpu

F32 = jnp.float32
BF16 = jnp.bfloat16

LANES = 128
SUBLANES = 8
MXU_WIDTH = 256
VMEM_LIMIT_BYTES = 56 * 1024 * 1024

HEAD_DIM = 64
PAIR = 2 * HEAD_DIM
CHUNK = 64
LEFT_CHUNKS = 8
BAND = (LEFT_CHUNKS + 1) * CHUNK
LEFT_PAD = LEFT_CHUNKS * CHUNK
MAX_REL = 128
DECAY_RANK = 64
ICLR_RANK = 64
GATE_RANK = 128
CONV_W = 3
NORM_EPS = 1e-6
GN_EPS = 64e-5
NEG_INF = -1e30
EXP_NEG_HALF = math.exp(-0.5)
ATTN_GROUP = 8

assert PAIR == LANES


def _rms_norm(x, g):
    return x * lax.rsqrt(jnp.mean(x * x, axis=-1, keepdims=True) + NORM_EPS) * g


def _resident(shape):
    return pl.BlockSpec(shape, lambda *_: (0,) * len(shape), pipeline_mode=pl.Buffered(1))


def _inproj_kernel(x_ref, g_ref, w_ref, mu_ref, gb_ref, zs_ref, qkv_ref, gate_ref, carry_ref,
                   *, blocks_per_seq, rwkv_cols, attn_cols):
    i = pl.program_id(0)
    rows = x_ref.shape[0]

    @pl.when(i % blocks_per_seq == 0)
    def _():
        carry_ref[...] = jnp.zeros_like(carry_ref)

    hb = _rms_norm(x_ref[...], g_ref[...]).astype(BF16)

    zg = jnp.dot(hb, w_ref[:, rwkv_cols + attn_cols:], preferred_element_type=F32)
    gate_ref[...] = jax.nn.sigmoid(zg + gb_ref[...]).astype(BF16)

    z = jnp.dot(hb, w_ref[:, :rwkv_cols], preferred_element_type=F32)
    row = lax.broadcasted_iota(jnp.int32, z.shape, 0)
    prev = jnp.where(row == 0, carry_ref[0:1, :], pltpu.roll(z, 1, 0))
    zs_ref[...] = z + (prev - z) * mu_ref[...]
    carry_ref[0:1, :] = z[rows - 1:rows, :]

    qkv_ref[...] = jnp.dot(hb, w_ref[:, rwkv_cols:rwkv_cols + attn_cols],
                           preferred_element_type=F32).astype(BF16)


def _inproj(x2, g, w_bf, mu, gate_b, *, seq, rwkv_cols, attn_cols, tm):
    m, d = x2.shape
    gate_cols = w_bf.shape[1] - rwkv_cols - attn_cols
    body = functools.partial(_inproj_kernel, blocks_per_seq=seq // tm,
                             rwkv_cols=rwkv_cols, attn_cols=attn_cols)
    return pl.pallas_call(
        body,
        grid=(m // tm,),
        in_specs=[
            pl.BlockSpec((tm, d), lambda i: (i, 0)),
            _resident((1, d)),
            _resident(w_bf.shape),
            _resident((1, rwkv_cols)),
            _resident((1, gate_cols)),
        ],
        out_specs=[
            pl.BlockSpec((tm, rwkv_cols), lambda i: (i, 0)),
            pl.BlockSpec((tm, attn_cols), lambda i: (i, 0)),
            pl.BlockSpec((tm, gate_cols), lambda i: (i, 0)),
        ],
        out_shape=[
            jax.ShapeDtypeStruct((m, rwkv_cols), F32),
            jax.ShapeDtypeStruct((m, attn_cols), BF16),
            jax.ShapeDtypeStruct((m, gate_cols), BF16),
        ],
        scratch_shapes=[pltpu.VMEM((SUBLANES, rwkv_cols), F32)],
        compiler_params=pltpu.CompilerParams(
            dimension_semantics=("arbitrary",), vmem_limit_bytes=VMEM_LIMIT_BYTES),
        name="inproj",
    )(x2, g, w_bf, mu, gate_b)


def _mm(a, b, dims=None):
    a = a.astype(BF16)
    b = b.astype(BF16)
    if dims is None:
        return jnp.dot(a, b, preferred_element_type=F32)
    return lax.dot_general(a, b, (dims, ((), ())), preferred_element_type=F32)


_NT = ((1,), (1,))
_TN = ((0,), (0,))


def _split2(x):
    hi = x.astype(BF16)
    return hi, (x - hi.astype(F32)).astype(BF16)


def _mm_split(a, b):
    rows = a.shape[0]
    a_hi, a_lo = _split2(a)
    b_hi, b_lo = _split2(b)
    both = jnp.dot(jnp.concatenate([a_hi, a_lo], axis=0), b_hi, preferred_element_type=F32)
    return both[:rows] + both[rows:] + jnp.dot(a_hi, b_lo, preferred_element_type=F32)


def _split3(x):
    hi = x.astype(BF16)
    r1 = x - hi.astype(F32)
    mid = r1.astype(BF16)
    return hi, mid, (r1 - mid.astype(F32)).astype(BF16)


def _unit_lower_inverse_minus_eye(mats, row, col, same_block):
    n = same_block.shape[0] // CHUNK

    def mul(xs, ys):
        out = []
        for x, y in zip(xs, ys):
            y_diag = jnp.where(same_block, jnp.concatenate([y.astype(BF16)] * n, axis=0),
                               jnp.zeros((), BF16))
            out.append(jnp.dot(x.astype(BF16), y_diag, preferred_element_type=F32))
        return out

    def same(width):
        return (row // width) == (col // width)

    m = [-jnp.where(same(2), a, 0.0) for a in mats]
    width = 2
    while width < CHUNK:
        below = jnp.logical_and(same(2 * width), jnp.logical_not(same(width)))
        off = [jnp.where(below, a, 0.0) for a in mats]
        z = [o + p for o, p in zip(off, mul(m, off))]
        m = [mi - zi - p for mi, zi, p in zip(m, z, mul(z, m))]
        width *= 2
    return m


def _rwkv_kernel(r_ref, k_ref, v_ref, wa_ref, gd_ref,
                 w0_ref, w2_ref, a0_ref, a2_ref, g2_ref, kk_ref, ka_ref, rk_ref, lnw_ref, lnb_ref,
                 o_ref, state_ref):
    tb, width = r_ref.shape
    n_chunks = tb // CHUNK
    pairs = range(width // PAIR)
    lanes = [slice(q * PAIR, (q + 1) * PAIR) for q in pairs]

    @pl.when(pl.program_id(1) == 0)
    def _():
        state_ref[...] = jnp.zeros_like(state_ref)

    lane = lax.broadcasted_iota(jnp.int32, (1, PAIR), 1)
    head0 = lane < HEAD_DIM
    rr = lax.broadcasted_iota(jnp.int32, (PAIR, PAIR), 0)
    cc = lax.broadcasted_iota(jnp.int32, (PAIR, PAIR), 1)
    same_head = (rr // HEAD_DIM) == (cc // HEAD_DIM)
    ones2 = jnp.concatenate([same_head.astype(BF16)] * 2, axis=0)

    def head_sum(t):
        return jnp.concatenate(
            [jnp.dot(jnp.concatenate(_split2(t[:, l]), axis=1), ones2, preferred_element_type=F32)
             for l in lanes], axis=1)

    r = r_ref[...]
    k = k_ref[...]
    v = v_ref[...]
    wa = wa_ref[...]
    u = w0_ref[...] + _mm(jnp.tanh(wa), w2_ref[...])
    ld = -EXP_NEG_HALF * jax.nn.sigmoid(u)
    a = jax.nn.sigmoid(a0_ref[...] + _mm(wa, a2_ref[...]))
    g = _mm(jax.nn.sigmoid(gd_ref[...]), g2_ref[...])
    kk = k * kk_ref[...]
    kk = kk * lax.rsqrt(jnp.maximum(head_sum(kk * kk), 1e-24))
    kp = k * (1.0 + (a - 1.0) * ka_ref[...])
    bonus = head_sum(r * kp * rk_ref[...]) * v

    wide = 2 * PAIR
    ti = lax.broadcasted_iota(jnp.int32, (CHUNK, wide), 0)
    si = lax.broadcasted_iota(jnp.int32, (CHUNK, wide), 1) % CHUNK
    lower_w = si <= ti
    strict_w = si < ti
    bi = lax.broadcasted_iota(jnp.int32, (wide, wide), 0)
    bj = lax.broadcasted_iota(jnp.int32, (wide, wide), 1)
    same_block = (bi // CHUNK) == (bj // CHUNK)
    head0_w = jnp.concatenate([head0, head0], axis=1)
    tri3 = jnp.concatenate([lower_w[:, :CHUNK].astype(BF16)] * 3, axis=1)

    def per_head(t):
        mask = head0 if t.shape[1] == PAIR else head0_w
        return jnp.concatenate([jnp.where(mask, t, 0.0), jnp.where(mask, 0.0, t)], axis=0)

    ld_parts = _split3(ld)
    cum = jnp.concatenate([
        jnp.dot(tri3, jnp.concatenate([part[c * CHUNK:(c + 1) * CHUNK] for part in ld_parts], axis=0),
                preferred_element_type=F32)
        for c in range(n_chunks)], axis=0)
    e_pos = jnp.exp(cum)
    e_neg = jnp.exp(-cum)
    rt_all = r * e_pos
    kt_all = kk * jnp.exp(cum - ld)
    bt_all = kk * a * e_neg
    kpt_all = kp * e_neg

    items = [(q, c) for q in pairs for c in range(n_chunks)]
    n = range(len(items))
    cut = lambda t: [t[c * CHUNK:(c + 1) * CHUNK, lanes[q]] for q, c in items]
    rt, kt, bt, kpt, vc = cut(rt_all), cut(kt_all), cut(bt_all), cut(kpt_all), cut(v)
    gamma_end = [t[CHUNK - 1:CHUNK, :] for t in cut(e_pos)]
    zeros = jnp.zeros((CHUNK, PAIR), F32)

    sc = [_mm(jnp.concatenate([kt[i], rt[i]], axis=0),
              jnp.concatenate([per_head(bt[i]), per_head(kpt[i])], axis=0), _NT) for i in n]
    a_bk = [jnp.where(strict_w, s[:CHUNK], 0.0) for s in sc]
    r_bk = [jnp.where(lower_w, s[CHUNK:], 0.0) for s in sc]
    akv = [_mm(a_bk[i][:, PAIR:], per_head(vc[i])) for i in n]
    t_m = _unit_lower_inverse_minus_eye(
        [jnp.concatenate([a_bk[i][:, :PAIR], a_bk[i + 1][:, :PAIR]], axis=1)
         for i in range(0, len(items), 2)], ti, si, same_block)
    t_m = [t_m[i // 2][:, (i % 2) * PAIR:(i % 2 + 1) * PAIR] for i in n]
    x = [jnp.concatenate([akv[i], kt[i]], axis=1) for i in n]
    fe = [x[i] + _mm_split(t_m[i], per_head(x[i])) for i in n]
    f_all = [t[:, :PAIR] for t in fe]
    e_all = [t[:, PAIR:] for t in fe]
    yy = [_mm(r_bk[i], jnp.concatenate([per_head(-fe[i]),
                                        per_head(jnp.concatenate([vc[i], zeros], axis=1))], axis=0))
          for i in n]
    y_own = [t[:, :PAIR] for t in yy]
    ry = [rt[i] + yy[i][:, PAIR:] for i in n]
    b_e = [jnp.where(same_head, _mm(bt[i], e_all[i], _TN), 0.0) for i in n]
    q_g = [jnp.where(same_head, _mm(jnp.concatenate([vc[i], -f_all[i]], axis=0),
                                    jnp.concatenate([kpt[i], bt[i]], axis=0), _TN), 0.0)
           for i in n]

    state = [state_ref[q] for q in pairs]
    y = [[None] * n_chunks for _ in pairs]
    for c in range(n_chunks):
        for q in pairs:
            i = q * n_chunks + c
            y[q][c] = _mm(ry[i], state[q], _NT) + y_own[i]
            state[q] = (state[q] - _mm(state[q], b_e[i], _NT) + q_g[i]) * gamma_end[i]
    for q in pairs:
        state_ref[q] = state[q]

    y = jnp.concatenate([jnp.concatenate(y[q], axis=0) for q in pairs], axis=1)
    mean = head_sum(y) * (1.0 / HEAD_DIM)
    yc = y - mean
    var = head_sum(yc * yc) * (1.0 / HEAD_DIM)
    yn = yc * lax.rsqrt(var + GN_EPS) * lnw_ref[...] + lnb_ref[...]
    o_ref[...] = ((yn + bonus) * g).astype(o_ref.dtype)


def _rwkv(zs3, w0, w2p, a0, a2p, g2, k_k, k_a, r_k, lnx_w, lnx_b, *, width, tb):
    nb, seq, _ = zs3.shape
    assert width % PAIR == 0 and (tb // CHUNK) % 2 == 0
    lora_tile = 3 * width // LANES
    gate_tile = lora_tile + 1

    def tok(cols, col_block):
        return pl.BlockSpec((None, tb, cols), lambda b, t: (b, t, col_block))

    return pl.pallas_call(
        _rwkv_kernel,
        grid=(nb, seq // tb),
        in_specs=[
            tok(width, 0), tok(width, 1), tok(width, 2), tok(LANES, lora_tile), tok(LANES, gate_tile),
            _resident(w0.shape), _resident(w2p.shape), _resident(a0.shape), _resident(a2p.shape),
            _resident(g2.shape), _resident(k_k.shape), _resident(k_a.shape), _resident(r_k.shape),
            _resident(lnx_w.shape), _resident(lnx_b.shape),
        ],
        out_specs=pl.BlockSpec((None, tb, width), lambda b, t: (b, t, 0)),
        out_shape=jax.ShapeDtypeStruct((nb, seq, width), BF16),
        scratch_shapes=[pltpu.VMEM((width // PAIR, PAIR, PAIR), F32)],
        compiler_params=pltpu.CompilerParams(
            dimension_semantics=("parallel", "arbitrary"), vmem_limit_bytes=VMEM_LIMIT_BYTES),
        name="rwkv",
    )(zs3, zs3, zs3, zs3, zs3, w0, w2p, a0, a2p, g2, k_k, k_a, r_k, lnx_w, lnx_b)


def _attn_kernel(q_ref, k_ref, v_ref, bias_ref, o_ref, kpad_ref, vpad_ref, *, scale, group):
    tq = q_ref.shape[0]
    seq = k_ref.shape[0]
    t = pl.program_id(2)

    @pl.when(t == 0)
    def _():
        zeros = jnp.zeros((LEFT_PAD, PAIR), kpad_ref.dtype)
        kpad_ref[0:LEFT_PAD, :] = zeros
        vpad_ref[0:LEFT_PAD, :] = zeros
        kpad_ref[LEFT_PAD:LEFT_PAD + seq, :] = k_ref[...]
        vpad_ref[LEFT_PAD:LEFT_PAD + seq, :] = v_ref[...]

    lane = lax.broadcasted_iota(jnp.int32, (1, PAIR), 1)
    head0 = lane < HEAD_DIM
    head_masks = (head0, jnp.logical_not(head0))
    col = lax.broadcasted_iota(jnp.int32, (CHUNK, BAND), 1)

    def group_body(g, first, masked):
        q, kb, vb = [], [], []
        for j in range(group):
            rows = pl.ds(pl.multiple_of((g * group + j) * CHUNK, CHUNK), CHUNK)
            band = pl.ds(pl.multiple_of((first + j) * CHUNK, CHUNK), BAND)
            q.append(q_ref[rows, :] * scale)
            kb.append(kpad_ref[band, :])
            vb.append(vpad_ref[band, :])
        inst = [(j, h) for j in range(group) for h in range(2)]
        s = [lax.dot_general(jnp.where(head_masks[h], q[j], jnp.zeros_like(q[j])), kb[j],
                             (((1,), (1,)), ((), ())), preferred_element_type=F32) + bias_ref[h]
             for j, h in inst]
        if masked:
            s = [jnp.where(col >= (LEFT_CHUNKS - first - j) * CHUNK, si, NEG_INF)
                 for si, (j, _) in zip(s, inst)]
        p = [jnp.exp(si - jnp.max(si, axis=-1, keepdims=True)) for si in s]
        denom = [jnp.sum(pi, axis=-1, keepdims=True) for pi in p]
        o = [jnp.dot(pi.astype(vb[j].dtype), vb[j], preferred_element_type=F32) / di
             for pi, di, (j, _) in zip(p, denom, inst)]
        for j in range(group):
            rows = pl.ds(pl.multiple_of((g * group + j) * CHUNK, CHUNK), CHUNK)
            o_ref[rows, :] = jnp.where(head0, o[2 * j], o[2 * j + 1]).astype(o_ref.dtype)

    def group_step(g, carry):
        first = t * (tq // CHUNK) + g * group
        lax.cond(first < LEFT_CHUNKS,
                 lambda: group_body(g, first, True), lambda: group_body(g, first, False))
        return carry

    lax.fori_loop(0, tq // (CHUNK * group), group_step, 0)


def _attention(qkv3, bias, *, width, tq):
    nb, seq, _ = qkv3.shape
    n_pairs = width // PAIR
    scale = HEAD_DIM ** -0.5
    assert math.frexp(scale)[0] == 0.5, "folding the scale into bf16 q is exact only for a power of two"
    return pl.pallas_call(
        functools.partial(_attn_kernel, scale=scale, group=ATTN_GROUP),
        grid=(nb, n_pairs, seq // tq),
        in_specs=[
            pl.BlockSpec((None, tq, PAIR), lambda b, p, t: (b, t, p)),
            pl.BlockSpec((None, seq, PAIR), lambda b, p, t: (b, 0, n_pairs + p)),
            pl.BlockSpec((None, seq, PAIR), lambda b, p, t: (b, 0, 2 * n_pairs + p)),
            pl.BlockSpec((2, CHUNK, BAND), lambda b, p, t: (p, 0, 0)),
        ],
        out_specs=pl.BlockSpec((None, tq, PAIR), lambda b, p, t: (b, t, p)),
        out_shape=jax.ShapeDtypeStruct((nb, seq, width), BF16),
        scratch_shapes=[pltpu.VMEM((LEFT_PAD + seq, PAIR), BF16)] * 2,
        compiler_params=pltpu.CompilerParams(
            dimension_semantics=("parallel", "parallel", "arbitrary"),
            vmem_limit_bytes=VMEM_LIMIT_BYTES),
        name="chunk_attention",
    )(qkv3, qkv3, qkv3, bias)


def _band_bias(rel_bias):
    period = BAND + CHUNK
    n = jnp.arange(period)
    j_minus_q = jnp.where(n < BAND, n, n - period)
    idx = jnp.clip(LEFT_PAD - j_minus_q, -(CHUNK - 1), MAX_REL) + (CHUNK - 1)
    diagonals = rel_bias.astype(F32)[:, idx]
    skewed = jnp.tile(diagonals, (1, CHUNK))[:, :CHUNK * (period - 1)]
    return skewed.reshape(-1, CHUNK, period - 1)[:, :, :BAND]


def _ff_tiles(d_ff):
    passes = d_ff // MXU_WIDTH
    if d_ff % MXU_WIDTH or passes < 2:
        return [(0, d_ff)]
    split = (passes + 1) // 2 * MXU_WIDTH
    return [(0, split), (split, d_ff)]


def _mix_kernel(x_ref, oa_ref, ob_ref, gate_ref, p_ref, wa_ref, wb_ref, wo_ref, g2_ref, wup_ref,
                cw_ref, cb_ref, wdn_ref, g3_ref, wple_ref, wpg_ref, bpg_ref, gf_ref, o_ref, carry_ref,
                *, blocks_per_seq, final_norm):
    i = pl.program_id(0)
    rows, d = x_ref.shape
    d_ff = wdn_ref.shape[0]

    @pl.when(i % blocks_per_seq == 0)
    def _():
        carry_ref[...] = jnp.zeros_like(carry_ref)

    br_a = jnp.dot(oa_ref[...], wa_ref[...], preferred_element_type=F32)
    br_b = jnp.dot(ob_ref[...], wb_ref[...], preferred_element_type=F32)
    merged = gate_ref[:, :d].astype(F32) * br_a + gate_ref[:, d:].astype(F32) * br_b
    x = x_ref[...] + jnp.dot(merged.astype(BF16), wo_ref[...], preferred_element_type=F32)

    hb = _rms_norm(x, g2_ref[...]).astype(BF16)
    acc = x
    for lo, hi in _ff_tiles(d_ff):
        cols = slice(lo, hi)
        row = lax.broadcasted_iota(jnp.int32, (rows, hi - lo), 0)
        a = jnp.dot(hb, wup_ref[:, cols], preferred_element_type=F32)
        gv = jnp.dot(hb, wup_ref[:, d_ff + lo:d_ff + hi], preferred_element_type=F32)
        c0 = carry_ref[0:1, cols]
        c1 = carry_ref[1:2, cols]
        prev1 = jnp.where(row == 0, c1, pltpu.roll(a, 1, 0))
        prev2 = jnp.where(row == 0, c0, jnp.where(row == 1, c1, pltpu.roll(a, 2, 0)))
        carry_ref[0:2, cols] = a[rows - 2:rows, :]
        conv = (cw_ref[0:1, cols] * prev2 + cw_ref[1:2, cols] * prev1 + cw_ref[2:3, cols] * a
                + cb_ref[:, cols])
        act = 0.5 * conv * (1.0 + lax.erf(conv * math.sqrt(0.5))) * gv
        acc = acc + jnp.dot(act.astype(BF16), wdn_ref[cols, :], preferred_element_type=F32)

    h3 = _rms_norm(acc, g3_ref[...]).astype(BF16)
    pg = jax.nn.sigmoid(jnp.dot(h3, wpg_ref[...], preferred_element_type=F32) + bpg_ref[...])
    ple = jnp.dot(p_ref[...].astype(BF16), wple_ref[...], preferred_element_type=F32)
    out = acc + ple * pg
    if final_norm:
        out = _rms_norm(out, gf_ref[...])
    o_ref[...] = out


def _mix(x2, oa, ob, gates, p2, w_a, w_b, w_o, g2, w_up, conv_w, conv_b, w_down, g3, w_ple, w_pg,
         b_pg, gf, *, seq, tm, final_norm):
    m, d = x2.shape
    d_ff = w_down.shape[0]
    body = functools.partial(_mix_kernel, blocks_per_seq=seq // tm, final_norm=final_norm)
    row = lambda t: pl.BlockSpec((tm, t.shape[1]), lambda i: (i, 0))
    weights = (w_a, w_b, w_o, g2, w_up, conv_w, conv_b, w_down, g3, w_ple, w_pg, b_pg, gf)
    return pl.pallas_call(
        body,
        grid=(m // tm,),
        in_specs=[row(x2), row(oa), row(ob), row(gates), row(p2)]
                 + [_resident(w.shape) for w in weights],
        out_specs=row(x2),
        out_shape=jax.ShapeDtypeStruct((m, d), F32),
        scratch_shapes=[pltpu.VMEM((SUBLANES, d_ff), F32)],
        compiler_params=pltpu.CompilerParams(
            dimension_semantics=("arbitrary",), vmem_limit_bytes=VMEM_LIMIT_BYTES),
        name="mix",
    )(x2, oa, ob, gates, p2, *weights)


def kernel(x, p, ln1_g, w_in, mix_mu, w0, w2, a0, a2, g2, k_k, k_a, r_k, lnx_w, lnx_b, rel_bias, gate_b, w_br_rwkv, w_br_attn, w_o, ln2_g, w_ffn_up, conv_w, conv_b, w_ffn_down, ln3_g, w_ple, w_pg, b_pg, lnf_g):
    nb, seq, d = x.shape
    depth = w_in.shape[0]
    rwkv_width = w_br_rwkv.shape[1]
    attn_width = w_br_attn.shape[1]
    rwkv_cols = 3 * rwkv_width + DECAY_RANK + ICLR_RANK + GATE_RANK
    attn_cols = 3 * attn_width
    assert DECAY_RANK + ICLR_RANK == LANES and GATE_RANK == LANES
    assert seq % 512 == 0 and seq >= BAND
    row = lambda t: t.reshape(1, -1)
    m = nb * seq
    x2 = x.reshape(m, d)

    for i in range(depth):
        last = i == depth - 1
        zs, qkv, gates = _inproj(
            x2, row(ln1_g[i]), w_in[i].astype(BF16), row(mix_mu[i]), row(gate_b[i]),
            seq=seq, rwkv_cols=rwkv_cols, attn_cols=attn_cols, tm=512)

        w2p = jnp.concatenate([w2[i], jnp.zeros_like(a2[i])], axis=0)
        a2p = jnp.concatenate([jnp.zeros_like(w2[i]), a2[i]], axis=0)
        o_a = _rwkv(zs.reshape(nb, seq, rwkv_cols), row(w0[i]), w2p, row(a0[i]), a2p, g2[i],
                    row(k_k[i]), row(k_a[i]), row(r_k[i]), row(lnx_w[i]), row(lnx_b[i]),
                    width=rwkv_width, tb=512)

        o_b = _attention(qkv.reshape(nb, seq, attn_cols), _band_bias(rel_bias[i]),
                         width=attn_width, tq=1024)

        x2 = _mix(x2, o_a.reshape(m, rwkv_width), o_b.reshape(m, attn_width), gates,
                  p[i].reshape(m, -1), w_br_rwkv[i].astype(BF16), w_br_attn[i].astype(BF16),
                  w_o[i].astype(BF16), row(ln2_g[i]), w_ffn_up[i].astype(BF16), conv_w[i],
                  row(conv_b[i]), w_ffn_down[i].astype(BF16), row(ln3_g[i]), w_ple[i].astype(BF16),
                  w_pg[i].astype(BF16), row(b_pg[i]), row(lnf_g), seq=seq, tm=512, final_norm=last)
    return x2.reshape(nb, seq, d)
```

```python
import functools
import math

import jax
import jax.numpy as jnp
from jax import lax
from jax.experimental import pallas as pl
from jax.experimental.pallas import tpu as pltpu

F32 = jnp.float32
BF16 = jnp.bfloat16

LANES = 128
SUBLANES = 8
MXU_WIDTH = 256
VMEM_LIMIT_BYTES = 56 * 1024 * 1024

HEAD_DIM = 64
PAIR = 2 * HEAD_DIM
CHUNK = 64
LEFT_CHUNKS = 8
BAND = (LEFT_CHUNKS + 1) * CHUNK
LEFT_PAD = LEFT_CHUNKS * CHUNK
MAX_REL = 128
DECAY_RANK = 64
ICLR_RANK = 64
GATE_RANK = 128
CONV_W = 3
NORM_EPS = 1e-6
GN_EPS = 64e-5
NEG_INF = -1e30
EXP_NEG_HALF = math.exp(-0.5)
ATTN_GROUP = 8

assert PAIR == LANES


def _rms_norm(x, g):
    return x * lax.rsqrt(jnp.mean(x * x, axis=-1, keepdims=True) + NORM_EPS) * g


def _resident(shape):
    return pl.BlockSpec(shape, lambda *_: (0,) * len(shape), pipeline_mode=pl.Buffered(1))


def _inproj_kernel(x_ref, g_ref, w_ref, mu_ref, gb_ref, zs_ref, qkv_ref, gate_ref, carry_ref,
                   *, blocks_per_seq, rwkv_cols, attn_cols):
    i = pl.program_id(0)
    rows = x_ref.shape[0]

    @pl.when(i % blocks_per_seq == 0)
    def _():
        carry_ref[...] = jnp.zeros_like(carry_ref)

    hb = _rms_norm(x_ref[...], g_ref[...]).astype(BF16)

    zg = jnp.dot(hb, w_ref[:, rwkv_cols + attn_cols:], preferred_element_type=F32)
    gate_ref[...] = jax.nn.sigmoid(zg + gb_ref[...]).astype(BF16)

    z = jnp.dot(hb, w_ref[:, :rwkv_cols], preferred_element_type=F32)
    row = lax.broadcasted_iota(jnp.int32, z.shape, 0)
    prev = jnp.where(row == 0, carry_ref[0:1, :], pltpu.roll(z, 1, 0))
    zs_ref[...] = z + (prev - z) * mu_ref[...]
    carry_ref[0:1, :] = z[rows - 1:rows, :]

    qkv_ref[...] = jnp.dot(hb, w_ref[:, rwkv_cols:rwkv_cols + attn_cols],
                           preferred_element_type=F32).astype(BF16)


def _inproj(x2, g, w_bf, mu, gate_b, *, seq, rwkv_cols, attn_cols, tm):
    m, d = x2.shape
    gate_cols = w_bf.shape[1] - rwkv_cols - attn_cols
    body = functools.partial(_inproj_kernel, blocks_per_seq=seq // tm,
                             rwkv_cols=rwkv_cols, attn_cols=attn_cols)
    return pl.pallas_call(
        body,
        grid=(m // tm,),
        in_specs=[
            pl.BlockSpec((tm, d), lambda i: (i, 0)),
            _resident((1, d)),
            _resident(w_bf.shape),
            _resident((1, rwkv_cols)),
            _resident((1, gate_cols)),
        ],
        out_specs=[
            pl.BlockSpec((tm, rwkv_cols), lambda i: (i, 0)),
            pl.BlockSpec((tm, attn_cols), lambda i: (i, 0)),
            pl.BlockSpec((tm, gate_cols), lambda i: (i, 0)),
        ],
        out_shape=[
            jax.ShapeDtypeStruct((m, rwkv_cols), F32),
            jax.ShapeDtypeStruct((m, attn_cols), BF16),
            jax.ShapeDtypeStruct((m, gate_cols), BF16),
        ],
        scratch_shapes=[pltpu.VMEM((SUBLANES, rwkv_cols), F32)],
        compiler_params=pltpu.CompilerParams(
            dimension_semantics=("arbitrary",), vmem_limit_bytes=VMEM_LIMIT_BYTES),
        name="inproj",
    )(x2, g, w_bf, mu, gate_b)


def _mm(a, b, dims=None):
    a = a.astype(BF16)
    b = b.astype(BF16)
    if dims is None:
        return jnp.dot(a, b, preferred_element_type=F32)
    return lax.dot_general(a, b, (dims, ((), ())), preferred_element_type=F32)


_NT = ((1,), (1,))
_TN = ((0,), (0,))


def _split2(x):
    hi = x.astype(BF16)
    return hi, (x - hi.astype(F32)).astype(BF16)


def _mm_split(a, b):
    rows = a.shape[0]
    a_hi, a_lo = _split2(a)
    b_hi, b_lo = _split2(b)
    both = jnp.dot(jnp.concatenate([a_hi, a_lo], axis=0), b_hi, preferred_element_type=F32)
    return both[:rows] + both[rows:] + jnp.dot(a_hi, b_lo, preferred_element_type=F32)


def _split3(x):
    hi = x.astype(BF16)
    r1 = x - hi.astype(F32)
    mid = r1.astype(BF16)
    return hi, mid, (r1 - mid.astype(F32)).astype(BF16)


def _unit_lower_inverse_minus_eye(mats, row, col, same_block):
    n = same_block.shape[0] // CHUNK

    def mul(xs, ys):
        out = []
        for x, y in zip(xs, ys):
            y_diag = jnp.where(same_block, jnp.concatenate([y.astype(BF16)] * n, axis=0),
                               jnp.zeros((), BF16))
            out.append(jnp.dot(x.astype(BF16), y_diag, preferred_element_type=F32))
        return out

    def same(width):
        return (row // width) == (col // width)

    m = [-jnp.where(same(2), a, 0.0) for a in mats]
    width = 2
    while width < CHUNK:
        below = jnp.logical_and(same(2 * width), jnp.logical_not(same(width)))
        off = [jnp.where(below, a, 0.0) for a in mats]
        z = [o + p for o, p in zip(off, mul(m, off))]
        m = [mi - zi - p for mi, zi, p in zip(m, z, mul(z, m))]
        width *= 2
    return m


def _rwkv_kernel(r_ref, k_ref, v_ref, wa_ref, gd_ref,
                 w0_ref, w2_ref, a0_ref, a2_ref, g2_ref, kk_ref, ka_ref, rk_ref, lnw_ref, lnb_ref,
                 o_ref, state_ref):
    tb, width = r_ref.shape
    n_chunks = tb // CHUNK
    pairs = range(width // PAIR)
    lanes = [slice(q * PAIR, (q + 1) * PAIR) for q in pairs]

    @pl.when(pl.program_id(1) == 0)
    def _():
        state_ref[...] = jnp.zeros_like(state_ref)

    lane = lax.broadcasted_iota(jnp.int32, (1, PAIR), 1)
    head0 = lane < HEAD_DIM
    rr = lax.broadcasted_iota(jnp.int32, (PAIR, PAIR), 0)
    cc = lax.broadcasted_iota(jnp.int32, (PAIR, PAIR), 1)
    same_head = (rr // HEAD_DIM) == (cc // HEAD_DIM)
    ones2 = jnp.concatenate([same_head.astype(BF16)] * 2, axis=0)

    def head_sum(t):
        return jnp.concatenate(
            [jnp.dot(jnp.concatenate(_split2(t[:, l]), axis=1), ones2, preferred_element_type=F32)
             for l in lanes], axis=1)

    r = r_ref[...]
    k = k_ref[...]
    v = v_ref[...]
    wa = wa_ref[...]
    u = w0_ref[...] + _mm(jnp.tanh(wa), w2_ref[...])
    ld = -EXP_NEG_HALF * jax.nn.sigmoid(u)
    a = jax.nn.sigmoid(a0_ref[...] + _mm(wa, a2_ref[...]))
    g = _mm(jax.nn.sigmoid(gd_ref[...]), g2_ref[...])
    kk = k * kk_ref[...]
    kk = kk * lax.rsqrt(jnp.maximum(head_sum(kk * kk), 1e-24))
    kp = k * (1.0 + (a - 1.0) * ka_ref[...])
    bonus = head_sum(r * kp * rk_ref[...]) * v

    wide = 2 * PAIR
    ti = lax.broadcasted_iota(jnp.int32, (CHUNK, wide), 0)
    si = lax.broadcasted_iota(jnp.int32, (CHUNK, wide), 1) % CHUNK
    lower_w = si <= ti
    strict_w = si < ti
    bi = lax.broadcasted_iota(jnp.int32, (wide, wide), 0)
    bj = lax.broadcasted_iota(jnp.int32, (wide, wide), 1)
    same_block = (bi // CHUNK) == (bj // CHUNK)
    head0_w = jnp.concatenate([head0, head0], axis=1)
    tri3 = jnp.concatenate([lower_w[:, :CHUNK].astype(BF16)] * 3, axis=1)

    def per_head(t):
        mask = head0 if t.shape[1] == PAIR else head0_w
        return jnp.concatenate([jnp.where(mask, t, 0.0), jnp.where(mask, 0.0, t)], axis=0)

    ld_parts = _split3(ld)
    cum = jnp.concatenate([
        jnp.dot(tri3, jnp.concatenate([part[c * CHUNK:(c + 1) * CHUNK] for part in ld_parts], axis=0),
                preferred_element_type=F32)
        for c in range(n_chunks)], axis=0)
    e_pos = jnp.exp(cum)
    e_neg = jnp.exp(-cum)
    rt_all = r * e_pos
    kt_all = kk * jnp.exp(cum - ld)
    bt_all = kk * a * e_neg
    kpt_all = kp * e_neg

    items = [(q, c) for q in pairs for c in range(n_chunks)]
    n = range(len(items))
    cut = lambda t: [t[c * CHUNK:(c + 1) * CHUNK, lanes[q]] for q, c in items]
    rt, kt, bt, kpt, vc = cut(rt_all), cut(kt_all), cut(bt_all), cut(kpt_all), cut(v)
    gamma_end = [t[CHUNK - 1:CHUNK, :] for t in cut(e_pos)]
    zeros = jnp.zeros((CHUNK, PAIR), F32)

    sc = [_mm(jnp.concatenate([kt[i], rt[i]], axis=0),
              jnp.concatenate([per_head(bt[i]), per_head(kpt[i])], axis=0), _NT) for i in n]
    a_bk = [jnp.where(strict_w, s[:CHUNK], 0.0) for s in sc]
    r_bk = [jnp.where(lower_w, s[CHUNK:], 0.0) for s in sc]
    akv = [_mm(a_bk[i][:, PAIR:], per_head(vc[i])) for i in n]
    t_m = _unit_lower_inverse_minus_eye(
        [jnp.concatenate([a_bk[i][:, :PAIR], a_bk[i + 1][:, :PAIR]], axis=1)
         for i in range(0, len(items), 2)], ti, si, same_block)
    t_m = [t_m[i // 2][:, (i % 2) * PAIR:(i % 2 + 1) * PAIR] for i in n]
    x = [jnp.concatenate([akv[i], kt[i]], axis=1) for i in n]
    fe = [x[i] + _mm_split(t_m[i], per_head(x[i])) for i in n]
    f_all = [t[:, :PAIR] for t in fe]
    e_all = [t[:, PAIR:] for t in fe]
    yy = [_mm(r_bk[i], jnp.concatenate([per_head(-fe[i]),
                                        per_head(jnp.concatenate([vc[i], zeros], axis=1))], axis=0))
          for i in n]
    y_own = [t[:, :PAIR] for t in yy]
    ry = [rt[i] + yy[i][:, PAIR:] for i in n]
    b_e = [jnp.where(same_head, _mm(bt[i], e_all[i], _TN), 0.0) for i in n]
    q_g = [jnp.where(same_head, _mm(jnp.concatenate([vc[i], -f_all[i]], axis=0),
                                    jnp.concatenate([kpt[i], bt[i]], axis=0), _TN), 0.0)
           for i in n]

    state = [state_ref[q] for q in pairs]
    y = [[None] * n_chunks for _ in pairs]
    for c in range(n_chunks):
        for q in pairs:
            i = q * n_chunks + c
            y[q][c] = _mm(ry[i], state[q], _NT) + y_own[i]
            state[q] = (state[q] - _mm(state[q], b_e[i], _NT) + q_g[i]) * gamma_end[i]
    for q in pairs:
        state_ref[q] = state[q]

    y = jnp.concatenate([jnp.concatenate(y[q], axis=0) for q in pairs], axis=1)
    mean = head_sum(y) * (1.0 / HEAD_DIM)
    yc = y - mean
    var = head_sum(yc * yc) * (1.0 / HEAD_DIM)
    yn = yc * lax.rsqrt(var + GN_EPS) * lnw_ref[...] + lnb_ref[...]
    o_ref[...] = ((yn + bonus) * g).astype(o_ref.dtype)


def _rwkv(zs3, w0, w2p, a0, a2p, g2, k_k, k_a, r_k, lnx_w, lnx_b, *, width, tb):
    nb, seq, _ = zs3.shape
    assert width % PAIR == 0 and (tb // CHUNK) % 2 == 0
    lora_tile = 3 * width // LANES
    gate_tile = lora_tile + 1

    def tok(cols, col_block):
        return pl.BlockSpec((None, tb, cols), lambda b, t: (b, t, col_block))

    return pl.pallas_call(
        _rwkv_kernel,
        grid=(nb, seq // tb),
        in_specs=[
            tok(width, 0), tok(width, 1), tok(width, 2), tok(LANES, lora_tile), tok(LANES, gate_tile),
            _resident(w0.shape), _resident(w2p.shape), _resident(a0.shape), _resident(a2p.shape),
            _resident(g2.shape), _resident(k_k.shape), _resident(k_a.shape), _resident(r_k.shape),
            _resident(lnx_w.shape), _resident(lnx_b.shape),
        ],
        out_specs=pl.BlockSpec((None, tb, width), lambda b, t: (b, t, 0)),
        out_shape=jax.ShapeDtypeStruct((nb, seq, width), BF16),
        scratch_shapes=[pltpu.VMEM((width // PAIR, PAIR, PAIR), F32)],
        compiler_params=pltpu.CompilerParams(
            dimension_semantics=("parallel", "arbitrary"), vmem_limit_bytes=VMEM_LIMIT_BYTES),
        name="rwkv",
    )(zs3, zs3, zs3, zs3, zs3, w0, w2p, a0, a2p, g2, k_k, k_a, r_k, lnx_w, lnx_b)


def _attn_kernel(q_ref, k_ref, v_ref, bias_ref, o_ref, kpad_ref, vpad_ref, *, scale, group):
    tq = q_ref.shape[0]
    seq = k_ref.shape[0]
    t = pl.program_id(2)

    @pl.when(t == 0)
    def _():
        zeros = jnp.zeros((LEFT_PAD, PAIR), kpad_ref.dtype)
        kpad_ref[0:LEFT_PAD, :] = zeros
        vpad_ref[0:LEFT_PAD, :] = zeros
        kpad_ref[LEFT_PAD:LEFT_PAD + seq, :] = k_ref[...]
        vpad_ref[LEFT_PAD:LEFT_PAD + seq, :] = v_ref[...]

    lane = lax.broadcasted_iota(jnp.int32, (1, PAIR), 1)
    head0 = lane < HEAD_DIM
    head_masks = (head0, jnp.logical_not(head0))
    col = lax.broadcasted_iota(jnp.int32, (CHUNK, BAND), 1)

    def group_body(g, first, masked):
        q, kb, vb = [], [], []
        for j in range(group):
            rows = pl.ds(pl.multiple_of((g * group + j) * CHUNK, CHUNK), CHUNK)
            band = pl.ds(pl.multiple_of((first + j) * CHUNK, CHUNK), BAND)
            q.append(q_ref[rows, :] * scale)
            kb.append(kpad_ref[band, :])
            vb.append(vpad_ref[band, :])
        inst = [(j, h) for j in range(group) for h in range(2)]
        s = [lax.dot_general(jnp.where(head_masks[h], q[j], jnp.zeros_like(q[j])), kb[j],
                             (((1,), (1,)), ((), ())), preferred_element_type=F32) + bias_ref[h]
             for j, h in inst]
        if masked:
            s = [jnp.where(col >= (LEFT_CHUNKS - first - j) * CHUNK, si, NEG_INF)
                 for si, (j, _) in zip(s, inst)]
        p = [jnp.exp(si - jnp.max(si, axis=-1, keepdims=True)) for si in s]
        denom = [jnp.sum(pi, axis=-1, keepdims=True) for pi in p]
        o = [jnp.dot(pi.astype(vb[j].dtype), vb[j], preferred_element_type=F32) / di
             for pi, di, (j, _) in zip(p, denom, inst)]
        for j in range(group):
            rows = pl.ds(pl.multiple_of((g * group + j) * CHUNK, CHUNK), CHUNK)
            o_ref[rows, :] = jnp.where(head0, o[2 * j], o[2 * j + 1]).astype(o_ref.dtype)

    def group_step(g, carry):
        first = t * (tq // CHUNK) + g * group
        lax.cond(first < LEFT_CHUNKS,
                 lambda: group_body(g, first, True), lambda: group_body(g, first, False))
        return carry

    lax.fori_loop(0, tq // (CHUNK * group), group_step, 0)


def _attention(qkv3, bias, *, width, tq):
    nb, seq, _ = qkv3.shape
    n_pairs = width // PAIR
    scale = HEAD_DIM ** -0.5
    assert math.frexp(scale)[0] == 0.5, "folding the scale into bf16 q is exact only for a power of two"
    return pl.pallas_call(
        functools.partial(_attn_kernel, scale=scale, group=ATTN_GROUP),
        grid=(nb, n_pairs, seq // tq),
        in_specs=[
            pl.BlockSpec((None, tq, PAIR), lambda b, p, t: (b, t, p)),
            pl.BlockSpec((None, seq, PAIR), lambda b, p, t: (b, 0, n_pairs + p)),
            pl.BlockSpec((None, seq, PAIR), lambda b, p, t: (b, 0, 2 * n_pairs + p)),
            pl.BlockSpec((2, CHUNK, BAND), lambda b, p, t: (p, 0, 0)),
        ],
        out_specs=pl.BlockSpec((None, tq, PAIR), lambda b, p, t: (b, t, p)),
        out_shape=jax.ShapeDtypeStruct((nb, seq, width), BF16),
        scratch_shapes=[pltpu.VMEM((LEFT_PAD + seq, PAIR), BF16)] * 2,
        compiler_params=pltpu.CompilerParams(
            dimension_semantics=("parallel", "parallel", "arbitrary"),
            vmem_limit_bytes=VMEM_LIMIT_BYTES),
        name="chunk_attention",
    )(qkv3, qkv3, qkv3, bias)


def _band_bias(rel_bias):
    period = BAND + CHUNK
    n = jnp.arange(period)
    j_minus_q = jnp.where(n < BAND, n, n - period)
    idx = jnp.clip(LEFT_PAD - j_minus_q, -(CHUNK - 1), MAX_REL) + (CHUNK - 1)
    diagonals = rel_bias.astype(F32)[:, idx]
    skewed = jnp.tile(diagonals, (1, CHUNK))[:, :CHUNK * (period - 1)]
    return skewed.reshape(-1, CHUNK, period - 1)[:, :, :BAND]


def _ff_tiles(d_ff):
    passes = d_ff // MXU_WIDTH
    if d_ff % MXU_WIDTH or passes < 2:
        return [(0, d_ff)]
    split = (passes + 1) // 2 * MXU_WIDTH
    return [(0, split), (split, d_ff)]


def _mix_kernel(x_ref, oa_ref, ob_ref, gate_ref, p_ref, wa_ref, wb_ref, wo_ref, g2_ref, wup_ref,
                cw_ref, cb_ref, wdn_ref, g3_ref, wple_ref, wpg_ref, bpg_ref, gf_ref, o_ref, carry_ref,
                *, blocks_per_seq, final_norm):
    i = pl.program_id(0)
    rows, d = x_ref.shape
    d_ff = wdn_ref.shape[0]

    @pl.when(i % blocks_per_seq == 0)
    def _():
        carry_ref[...] = jnp.zeros_like(carry_ref)

    br_a = jnp.dot(oa_ref[...], wa_ref[...], preferred_element_type=F32)
    br_b = jnp.dot(ob_ref[...], wb_ref[...], preferred_element_type=F32)
    merged = gate_ref[:, :d].astype(F32) * br_a + gate_ref[:, d:].astype(F32) * br_b
    x = x_ref[...] + jnp.dot(merged.astype(BF16), wo_ref[...], preferred_element_type=F32)

    hb = _rms_norm(x, g2_ref[...]).astype(BF16)
    acc = x
    for lo, hi in _ff_tiles(d_ff):
        cols = slice(lo, hi)
        row = lax.broadcasted_iota(jnp.int32, (rows, hi - lo), 0)
        a = jnp.dot(hb, wup_ref[:, cols], preferred_element_type=F32)
        gv = jnp.dot(hb, wup_ref[:, d_ff + lo:d_ff + hi], preferred_element_type=F32)
        c0 = carry_ref[0:1, cols]
        c1 = carry_ref[1:2, cols]
        prev1 = jnp.where(row == 0, c1, pltpu.roll(a, 1, 0))
        prev2 = jnp.where(row == 0, c0, jnp.where(row == 1, c1, pltpu.roll(a, 2, 0)))
        carry_ref[0:2, cols] = a[rows - 2:rows, :]
        conv = (cw_ref[0:1, cols] * prev2 + cw_ref[1:2, cols] * prev1 + cw_ref[2:3, cols] * a
                + cb_ref[:, cols])
        act = 0.5 * conv * (1.0 + lax.erf(conv * math.sqrt(0.5))) * gv
        acc = acc + jnp.dot(act.astype(BF16), wdn_ref[cols, :], preferred_element_type=F32)

    h3 = _rms_norm(acc, g3_ref[...]).astype(BF16)
    pg = jax.nn.sigmoid(jnp.dot(h3, wpg_ref[...], preferred_element_type=F32) + bpg_ref[...])
    ple = jnp.dot(p_ref[...].astype(BF16), wple_ref[...], preferred_element_type=F32)
    out = acc + ple * pg
    if final_norm:
        out = _rms_norm(out, gf_ref[...])
    o_ref[...] = out


def _mix(x2, oa, ob, gates, p2, w_a, w_b, w_o, g2, w_up, conv_w, conv_b, w_down, g3, w_ple, w_pg,
         b_pg, gf, *, seq, tm, final_norm):
    m, d = x2.shape
    d_ff = w_down.shape[0]
    assert conv_w.shape[0] == CONV_W, "the kernel body unrolls the causal conv for this width"
    body = functools.partial(_mix_kernel, blocks_per_seq=seq // tm, final_norm=final_norm)
    row = lambda t: pl.BlockSpec((tm, t.shape[1]), lambda i: (i, 0))
    weights = (w_a, w_b, w_o, g2, w_up, conv_w, conv_b, w_down, g3, w_ple, w_pg, b_pg, gf)
    return pl.pallas_call(
        body,
        grid=(m // tm,),
        in_specs=[row(x2), row(oa), row(ob), row(gates), row(p2)]
                 + [_resident(w.shape) for w in weights],
        out_specs=row(x2),
        out_shape=jax.ShapeDtypeStruct((m, d), F32),
        scratch_shapes=[pltpu.VMEM((SUBLANES, d_ff), F32)],
        compiler_params=pltpu.CompilerParams(
            dimension_semantics=("arbitrary",), vmem_limit_bytes=VMEM_LIMIT_BYTES),
        name="mix",
    )(x2, oa, ob, gates, p2, *weights)


def kernel(x, p, ln1_g, w_in, mix_mu, w0, w2, a0, a2, g2, k_k, k_a, r_k, lnx_w, lnx_b, rel_bias, gate_b, w_br_rwkv, w_br_attn, w_o, ln2_g, w_ffn_up, conv_w, conv_b, w_ffn_down, ln3_g, w_ple, w_pg, b_pg, lnf_g):
    nb, seq, d = x.shape
    depth = w_in.shape[0]
    rwkv_width = w_br_rwkv.shape[1]
    attn_width = w_br_attn.shape[1]
    rwkv_cols = 3 * rwkv_width + DECAY_RANK + ICLR_RANK + GATE_RANK
    attn_cols = 3 * attn_width
    assert DECAY_RANK + ICLR_RANK == LANES and GATE_RANK == LANES
    attn_rows = min(seq, 2048)
    assert seq % 512 == 0 and seq % attn_rows == 0 and seq >= BAND
    row = lambda t: t.reshape(1, -1)
    m = nb * seq
    x2 = x.reshape(m, d)

    for i in range(depth):
        last = i == depth - 1
        zs, qkv, gates = _inproj(
            x2, row(ln1_g[i]), w_in[i].astype(BF16), row(mix_mu[i]), row(gate_b[i]),
            seq=seq, rwkv_cols=rwkv_cols, attn_cols=attn_cols, tm=512)

        w2p = jnp.concatenate([w2[i], jnp.zeros_like(a2[i])], axis=0)
        a2p = jnp.concatenate([jnp.zeros_like(w2[i]), a2[i]], axis=0)
        o_a = _rwkv(zs.reshape(nb, seq, rwkv_cols), row(w0[i]), w2p, row(a0[i]), a2p, g2[i],
                    row(k_k[i]), row(k_a[i]), row(r_k[i]), row(lnx_w[i]), row(lnx_b[i]),
                    width=rwkv_width, tb=512)

        o_b = _attention(qkv.reshape(nb, seq, attn_cols), _band_bias(rel_bias[i]),
                         width=attn_width, tq=attn_rows)

        x2 = _mix(x2, o_a.reshape(m, rwkv_width), o_b.reshape(m, attn_width), gates,
                  p[i].reshape(m, -1), w_br_rwkv[i].astype(BF16), w_br_attn[i].astype(BF16),
                  w_o[i].astype(BF16), row(ln2_g[i]), w_ffn_up[i].astype(BF16), conv_w[i],
                  row(conv_b[i]), w_ffn_down[i].astype(BF16), row(ln3_g[i]), w_ple[i].astype(BF16),
                  w_pg[i].astype(BF16), row(b_pg[i]), row(lnf_g), seq=seq, tm=512, final_norm=last)
    return x2.reshape(nb, seq, d)
```

```python
import functools
import math

import jax
import jax.numpy as jnp
from jax import lax
from jax.experimental import pallas as pl
from jax.experimental.pallas import tpu as pltpu

F32 = jnp.float32
BF16 = jnp.bfloat16

LANES = 128
SUBLANES = 8
MXU_WIDTH = 256
VMEM_LIMIT_BYTES = 56 * 1024 * 1024

HEAD_DIM = 64
PAIR = 2 * HEAD_DIM
CHUNK = 64
LEFT_CHUNKS = 8
BAND = (LEFT_CHUNKS + 1) * CHUNK
LEFT_PAD = LEFT_CHUNKS * CHUNK
MAX_REL = 128
DECAY_RANK = 64
ICLR_RANK = 64
GATE_RANK = 128
CONV_W = 3
NORM_EPS = 1e-6
GN_EPS = 64e-5
NEG_INF = -1e30
EXP_NEG_HALF = math.exp(-0.5)
ATTN_GROUP = 8

assert PAIR == LANES


def _rms_norm(x, g):
    return x * lax.rsqrt(jnp.mean(x * x, axis=-1, keepdims=True) + NORM_EPS) * g


def _resident(shape):
    return pl.BlockSpec(shape, lambda *_: (0,) * len(shape), pipeline_mode=pl.Buffered(1))


def _inproj_kernel(x_ref, g_ref, w_ref, mu_ref, gb_ref, zs_ref, qkv_ref, gate_ref, carry_ref,
                   *, blocks_per_seq, rwkv_cols, attn_cols):
    i = pl.program_id(0)
    rows = x_ref.shape[0]

    @pl.when(i % blocks_per_seq == 0)
    def _():
        carry_ref[...] = jnp.zeros_like(carry_ref)

    hb = _rms_norm(x_ref[...], g_ref[...]).astype(BF16)

    zg = jnp.dot(hb, w_ref[:, rwkv_cols + attn_cols:], preferred_element_type=F32)
    gate_ref[...] = jax.nn.sigmoid(zg + gb_ref[...]).astype(BF16)

    z = jnp.dot(hb, w_ref[:, :rwkv_cols], preferred_element_type=F32)
    row = lax.broadcasted_iota(jnp.int32, z.shape, 0)
    prev = jnp.where(row == 0, carry_ref[0:1, :], pltpu.roll(z, 1, 0))
    zs_ref[...] = z + (prev - z) * mu_ref[...]
    carry_ref[0:1, :] = z[rows - 1:rows, :]

    qkv = jnp.dot(hb, w_ref[:, rwkv_cols:rwkv_cols + attn_cols],
                  preferred_element_type=F32).astype(BF16)
    for j in range(attn_cols // LANES):
        qkv_ref[j] = qkv[:, j * LANES:(j + 1) * LANES]


def _inproj(x2, g, w_bf, mu, gate_b, *, seq, rwkv_cols, attn_cols, tm):
    m, d = x2.shape
    gate_cols = w_bf.shape[1] - rwkv_cols - attn_cols
    body = functools.partial(_inproj_kernel, blocks_per_seq=seq // tm,
                             rwkv_cols=rwkv_cols, attn_cols=attn_cols)
    return pl.pallas_call(
        body,
        grid=(m // tm,),
        in_specs=[
            pl.BlockSpec((tm, d), lambda i: (i, 0)),
            _resident((1, d)),
            _resident(w_bf.shape),
            _resident((1, rwkv_cols)),
            _resident((1, gate_cols)),
        ],
        out_specs=[
            pl.BlockSpec((tm, rwkv_cols), lambda i: (i, 0)),
            pl.BlockSpec((attn_cols // LANES, tm, LANES), lambda i: (0, i, 0)),
            pl.BlockSpec((tm, gate_cols), lambda i: (i, 0)),
        ],
        out_shape=[
            jax.ShapeDtypeStruct((m, rwkv_cols), F32),
            jax.ShapeDtypeStruct((attn_cols // LANES, m, LANES), BF16),
            jax.ShapeDtypeStruct((m, gate_cols), BF16),
        ],
        scratch_shapes=[pltpu.VMEM((SUBLANES, rwkv_cols), F32)],
        compiler_params=pltpu.CompilerParams(
            dimension_semantics=("arbitrary",), vmem_limit_bytes=VMEM_LIMIT_BYTES),
        name="inproj",
    )(x2, g, w_bf, mu, gate_b)


def _mm(a, b, dims=None):
    a = a.astype(BF16)
    b = b.astype(BF16)
    if dims is None:
        return jnp.dot(a, b, preferred_element_type=F32)
    return lax.dot_general(a, b, (dims, ((), ())), preferred_element_type=F32)


_NT = ((1,), (1,))
_TN = ((0,), (0,))


def _split2(x):
    hi = x.astype(BF16)
    return hi, (x - hi.astype(F32)).astype(BF16)


def _mm_split(a, b):
    rows = a.shape[0]
    a_hi, a_lo = _split2(a)
    b_hi, b_lo = _split2(b)
    both = jnp.dot(jnp.concatenate([a_hi, a_lo], axis=0), b_hi, preferred_element_type=F32)
    return both[:rows] + both[rows:] + jnp.dot(a_hi, b_lo, preferred_element_type=F32)


def _split3(x):
    hi = x.astype(BF16)
    r1 = x - hi.astype(F32)
    mid = r1.astype(BF16)
    return hi, mid, (r1 - mid.astype(F32)).astype(BF16)


def _unit_lower_inverse_minus_eye(mats, row, col, same_block):
    n = same_block.shape[0] // CHUNK

    def mul(xs, ys):
        out = []
        for x, y in zip(xs, ys):
            y_diag = jnp.where(same_block, jnp.concatenate([y.astype(BF16)] * n, axis=0),
                               jnp.zeros((), BF16))
            out.append(jnp.dot(x.astype(BF16), y_diag, preferred_element_type=F32))
        return out

    def same(width):
        return (row // width) == (col // width)

    m = [-jnp.where(same(2), a, 0.0) for a in mats]
    width = 2
    while width < CHUNK:
        below = jnp.logical_and(same(2 * width), jnp.logical_not(same(width)))
        off = [jnp.where(below, a, 0.0) for a in mats]
        z = [o + p for o, p in zip(off, mul(m, off))]
        m = [mi - zi - p for mi, zi, p in zip(m, z, mul(z, m))]
        width *= 2
    return m


def _rwkv_kernel(r_ref, k_ref, v_ref, wa_ref, gd_ref,
                 w0_ref, w2_ref, a0_ref, a2_ref, g2_ref, kk_ref, ka_ref, rk_ref, lnw_ref, lnb_ref,
                 o_ref, state_ref):
    tb, width = r_ref.shape
    n_chunks = tb // CHUNK
    pairs = range(width // PAIR)
    lanes = [slice(q * PAIR, (q + 1) * PAIR) for q in pairs]

    @pl.when(pl.program_id(1) == 0)
    def _():
        state_ref[...] = jnp.zeros_like(state_ref)

    lane = lax.broadcasted_iota(jnp.int32, (1, PAIR), 1)
    head0 = lane < HEAD_DIM
    rr = lax.broadcasted_iota(jnp.int32, (PAIR, PAIR), 0)
    cc = lax.broadcasted_iota(jnp.int32, (PAIR, PAIR), 1)
    same_head = (rr // HEAD_DIM) == (cc // HEAD_DIM)
    ones2 = jnp.concatenate([same_head.astype(BF16)] * 2, axis=0)

    def head_sum(t):
        return jnp.concatenate(
            [jnp.dot(jnp.concatenate(_split2(t[:, l]), axis=1), ones2, preferred_element_type=F32)
             for l in lanes], axis=1)

    r = r_ref[...]
    k = k_ref[...]
    v = v_ref[...]
    wa = wa_ref[...]
    u = w0_ref[...] + _mm(jnp.tanh(wa), w2_ref[...])
    ld = -EXP_NEG_HALF * jax.nn.sigmoid(u)
    a = jax.nn.sigmoid(a0_ref[...] + _mm(wa, a2_ref[...]))
    g = _mm(jax.nn.sigmoid(gd_ref[...]), g2_ref[...])
    kk = k * kk_ref[...]
    kk = kk * lax.rsqrt(jnp.maximum(head_sum(kk * kk), 1e-24))
    kp = k * (1.0 + (a - 1.0) * ka_ref[...])
    bonus = head_sum(r * kp * rk_ref[...]) * v

    wide = 2 * PAIR
    ti = lax.broadcasted_iota(jnp.int32, (CHUNK, wide), 0)
    si = lax.broadcasted_iota(jnp.int32, (CHUNK, wide), 1) % CHUNK
    lower_w = si <= ti
    strict_w = si < ti
    bi = lax.broadcasted_iota(jnp.int32, (wide, wide), 0)
    bj = lax.broadcasted_iota(jnp.int32, (wide, wide), 1)
    same_block = (bi // CHUNK) == (bj // CHUNK)
    head0_w = jnp.concatenate([head0, head0], axis=1)
    tri3 = jnp.concatenate([lower_w[:, :CHUNK].astype(BF16)] * 3, axis=1)

    def per_head(t):
        mask = head0 if t.shape[1] == PAIR else head0_w
        return jnp.concatenate([jnp.where(mask, t, 0.0), jnp.where(mask, 0.0, t)], axis=0)

    ld_parts = _split3(ld)
    cum = jnp.concatenate([
        jnp.dot(tri3, jnp.concatenate([part[c * CHUNK:(c + 1) * CHUNK] for part in ld_parts], axis=0),
                preferred_element_type=F32)
        for c in range(n_chunks)], axis=0)
    e_pos = jnp.exp(cum)
    e_neg = jnp.exp(-cum)
    rt_all = r * e_pos
    kt_all = kk * jnp.exp(cum - ld)
    bt_all = kk * a * e_neg
    kpt_all = kp * e_neg

    items = [(q, c) for q in pairs for c in range(n_chunks)]
    n = range(len(items))
    cut = lambda t: [t[c * CHUNK:(c + 1) * CHUNK, lanes[q]] for q, c in items]
    rt, kt, bt, kpt, vc = cut(rt_all), cut(kt_all), cut(bt_all), cut(kpt_all), cut(v)
    gamma_end = [t[CHUNK - 1:CHUNK, :] for t in cut(e_pos)]
    zeros = jnp.zeros((CHUNK, PAIR), F32)

    sc = [_mm(jnp.concatenate([kt[i], rt[i]], axis=0),
              jnp.concatenate([per_head(bt[i]), per_head(kpt[i])], axis=0), _NT) for i in n]
    a_bk = [jnp.where(strict_w, s[:CHUNK], 0.0) for s in sc]
    r_bk = [jnp.where(lower_w, s[CHUNK:], 0.0) for s in sc]
    akv = [_mm(a_bk[i][:, PAIR:], per_head(vc[i])) for i in n]
    t_m = _unit_lower_inverse_minus_eye(
        [jnp.concatenate([a_bk[i][:, :PAIR], a_bk[i + 1][:, :PAIR]], axis=1)
         for i in range(0, len(items), 2)], ti, si, same_block)
    t_m = [t_m[i // 2][:, (i % 2) * PAIR:(i % 2 + 1) * PAIR] for i in n]
    x = [jnp.concatenate([akv[i], kt[i]], axis=1) for i in n]
    fe = [x[i] + _mm_split(t_m[i], per_head(x[i])) for i in n]
    f_all = [t[:, :PAIR] for t in fe]
    e_all = [t[:, PAIR:] for t in fe]
    yy = [_mm(r_bk[i], jnp.concatenate([per_head(-fe[i]),
                                        per_head(jnp.concatenate([vc[i], zeros], axis=1))], axis=0))
          for i in n]
    y_own = [t[:, :PAIR] for t in yy]
    ry = [rt[i] + yy[i][:, PAIR:] for i in n]
    b_e = [jnp.where(same_head, _mm(bt[i], e_all[i], _TN), 0.0) for i in n]
    q_g = [jnp.where(same_head, _mm(jnp.concatenate([vc[i], -f_all[i]], axis=0),
                                    jnp.concatenate([kpt[i], bt[i]], axis=0), _TN), 0.0)
           for i in n]

    state = [state_ref[q] for q in pairs]
    y = [[None] * n_chunks for _ in pairs]
    for c in range(n_chunks):
        for q in pairs:
            i = q * n_chunks + c
            y[q][c] = _mm(ry[i], state[q], _NT) + y_own[i]
            state[q] = (state[q] - _mm(state[q], b_e[i], _NT) + q_g[i]) * gamma_end[i]
    for q in pairs:
        state_ref[q] = state[q]

    y = jnp.concatenate([jnp.concatenate(y[q], axis=0) for q in pairs], axis=1)
    mean = head_sum(y) * (1.0 / HEAD_DIM)
    yc = y - mean
    var = head_sum(yc * yc) * (1.0 / HEAD_DIM)
    yn = yc * lax.rsqrt(var + GN_EPS) * lnw_ref[...] + lnb_ref[...]
    o_ref[...] = ((yn + bonus) * g).astype(o_ref.dtype)


def _rwkv(zs3, w0, w2p, a0, a2p, g2, k_k, k_a, r_k, lnx_w, lnx_b, *, width, tb):
    nb, seq, _ = zs3.shape
    assert width % PAIR == 0 and (tb // CHUNK) % 2 == 0
    lora_tile = 3 * width // LANES
    gate_tile = lora_tile + 1

    def tok(cols, col_block):
        return pl.BlockSpec((None, tb, cols), lambda b, t: (b, t, col_block))

    return pl.pallas_call(
        _rwkv_kernel,
        grid=(nb, seq // tb),
        in_specs=[
            tok(width, 0), tok(width, 1), tok(width, 2), tok(LANES, lora_tile), tok(LANES, gate_tile),
            _resident(w0.shape), _resident(w2p.shape), _resident(a0.shape), _resident(a2p.shape),
            _resident(g2.shape), _resident(k_k.shape), _resident(k_a.shape), _resident(r_k.shape),
            _resident(lnx_w.shape), _resident(lnx_b.shape),
        ],
        out_specs=pl.BlockSpec((None, tb, width), lambda b, t: (b, t, 0)),
        out_shape=jax.ShapeDtypeStruct((nb, seq, width), BF16),
        scratch_shapes=[pltpu.VMEM((width // PAIR, PAIR, PAIR), F32)],
        compiler_params=pltpu.CompilerParams(
            dimension_semantics=("parallel", "arbitrary"), vmem_limit_bytes=VMEM_LIMIT_BYTES),
        name="rwkv",
    )(zs3, zs3, zs3, zs3, zs3, w0, w2p, a0, a2p, g2, k_k, k_a, r_k, lnx_w, lnx_b)


def _attn_kernel(q_ref, k_ref, v_ref, bias_ref, o_ref, kpad_ref, vpad_ref, *, scale, group):
    tq = q_ref.shape[0]
    seq = k_ref.shape[0]
    t = pl.program_id(2)

    @pl.when(t == 0)
    def _():
        zeros = jnp.zeros((LEFT_PAD, PAIR), kpad_ref.dtype)
        kpad_ref[0:LEFT_PAD, :] = zeros
        vpad_ref[0:LEFT_PAD, :] = zeros
        kpad_ref[LEFT_PAD:LEFT_PAD + seq, :] = k_ref[...]
        vpad_ref[LEFT_PAD:LEFT_PAD + seq, :] = v_ref[...]

    lane = lax.broadcasted_iota(jnp.int32, (1, PAIR), 1)
    head0 = lane < HEAD_DIM
    head_masks = (head0, jnp.logical_not(head0))
    col = lax.broadcasted_iota(jnp.int32, (CHUNK, BAND), 1)

    def group_body(g, first, masked):
        q, kb, vb = [], [], []
        for j in range(group):
            rows = pl.ds(pl.multiple_of((g * group + j) * CHUNK, CHUNK), CHUNK)
            band = pl.ds(pl.multiple_of((first + j) * CHUNK, CHUNK), BAND)
            q.append(q_ref[rows, :] * scale)
            kb.append(kpad_ref[band, :])
            vb.append(vpad_ref[band, :])
        inst = [(j, h) for j in range(group) for h in range(2)]
        s = [lax.dot_general(jnp.where(head_masks[h], q[j], jnp.zeros_like(q[j])), kb[j],
                             (((1,), (1,)), ((), ())), preferred_element_type=F32) + bias_ref[h]
             for j, h in inst]
        if masked:
            s = [jnp.where(col >= (LEFT_CHUNKS - first - j) * CHUNK, si, NEG_INF)
                 for si, (j, _) in zip(s, inst)]
        p = [jnp.exp(si - jnp.max(si, axis=-1, keepdims=True)) for si in s]
        denom = [jnp.sum(pi, axis=-1, keepdims=True) for pi in p]
        o = [jnp.dot(pi.astype(vb[j].dtype), vb[j], preferred_element_type=F32) / di
             for pi, di, (j, _) in zip(p, denom, inst)]
        for j in range(group):
            rows = pl.ds(pl.multiple_of((g * group + j) * CHUNK, CHUNK), CHUNK)
            o_ref[rows, :] = jnp.where(head0, o[2 * j], o[2 * j + 1]).astype(o_ref.dtype)

    def group_step(g, carry):
        first = t * (tq // CHUNK) + g * group
        lax.cond(first < LEFT_CHUNKS,
                 lambda: group_body(g, first, True), lambda: group_body(g, first, False))
        return carry

    lax.fori_loop(0, tq // (CHUNK * group), group_step, 0)


def _attention(qkv4, bias, *, width, tq):
    _, nb, seq, _ = qkv4.shape
    n_pairs = width // PAIR
    scale = HEAD_DIM ** -0.5
    assert math.frexp(scale)[0] == 0.5, "folding the scale into bf16 q is exact only for a power of two"
    return pl.pallas_call(
        functools.partial(_attn_kernel, scale=scale, group=ATTN_GROUP),
        grid=(nb, n_pairs, seq // tq),
        in_specs=[
            pl.BlockSpec((None, None, tq, PAIR), lambda b, p, t: (p, b, t, 0)),
            pl.BlockSpec((None, None, seq, PAIR), lambda b, p, t: (n_pairs + p, b, 0, 0)),
            pl.BlockSpec((None, None, seq, PAIR), lambda b, p, t: (2 * n_pairs + p, b, 0, 0)),
            pl.BlockSpec((2, CHUNK, BAND), lambda b, p, t: (p, 0, 0)),
        ],
        out_specs=pl.BlockSpec((None, tq, PAIR), lambda b, p, t: (b, t, p)),
        out_shape=jax.ShapeDtypeStruct((nb, seq, width), BF16),
        scratch_shapes=[pltpu.VMEM((LEFT_PAD + seq, PAIR), BF16)] * 2,
        compiler_params=pltpu.CompilerParams(
            dimension_semantics=("parallel", "parallel", "arbitrary"),
            vmem_limit_bytes=VMEM_LIMIT_BYTES),
        name="chunk_attention",
    )(qkv4, qkv4, qkv4, bias)


def _band_bias(rel_bias):
    period = BAND + CHUNK
    n = jnp.arange(period)
    j_minus_q = jnp.where(n < BAND, n, n - period)
    idx = jnp.clip(LEFT_PAD - j_minus_q, -(CHUNK - 1), MAX_REL) + (CHUNK - 1)
    diagonals = rel_bias.astype(F32)[:, idx]
    skewed = jnp.tile(diagonals, (1, CHUNK))[:, :CHUNK * (period - 1)]
    return skewed.reshape(-1, CHUNK, period - 1)[:, :, :BAND]


def _ff_tiles(d_ff):
    passes = d_ff // MXU_WIDTH
    if d_ff % MXU_WIDTH or passes < 2:
        return [(0, d_ff)]
    split = (passes + 1) // 2 * MXU_WIDTH
    return [(0, split), (split, d_ff)]


def _mix_kernel(x_ref, oa_ref, ob_ref, gate_ref, p_ref, wa_ref, wb_ref, wo_ref, g2_ref, wup_ref,
                cw_ref, cb_ref, wdn_ref, g3_ref, wple_ref, wpg_ref, bpg_ref, gf_ref, o_ref, carry_ref,
                *, blocks_per_seq, final_norm):
    i = pl.program_id(0)
    rows, d = x_ref.shape
    d_ff = wdn_ref.shape[0]

    @pl.when(i % blocks_per_seq == 0)
    def _():
        carry_ref[...] = jnp.zeros_like(carry_ref)

    br_a = jnp.dot(oa_ref[...], wa_ref[...], preferred_element_type=F32)
    br_b = jnp.dot(ob_ref[...], wb_ref[...], preferred_element_type=F32)
    merged = gate_ref[:, :d].astype(F32) * br_a + gate_ref[:, d:].astype(F32) * br_b
    x = x_ref[...] + jnp.dot(merged.astype(BF16), wo_ref[...], preferred_element_type=F32)

    hb = _rms_norm(x, g2_ref[...]).astype(BF16)
    acc = x
    for lo, hi in _ff_tiles(d_ff):
        cols = slice(lo, hi)
        row = lax.broadcasted_iota(jnp.int32, (rows, hi - lo), 0)
        a = jnp.dot(hb, wup_ref[:, cols], preferred_element_type=F32)
        gv = jnp.dot(hb, wup_ref[:, d_ff + lo:d_ff + hi], preferred_element_type=F32)
        c0 = carry_ref[0:1, cols]
        c1 = carry_ref[1:2, cols]
        prev1 = jnp.where(row == 0, c1, pltpu.roll(a, 1, 0))
        prev2 = jnp.where(row == 0, c0, jnp.where(row == 1, c1, pltpu.roll(a, 2, 0)))
        carry_ref[0:2, cols] = a[rows - 2:rows, :]
        conv = (cw_ref[0:1, cols] * prev2 + cw_ref[1:2, cols] * prev1 + cw_ref[2:3, cols] * a
                + cb_ref[:, cols])
        act = 0.5 * conv * (1.0 + lax.erf(conv * math.sqrt(0.5))) * gv
        acc = acc + jnp.dot(act.astype(BF16), wdn_ref[cols, :], preferred_element_type=F32)

    h3 = _rms_norm(acc, g3_ref[...]).astype(BF16)
    pg = jax.nn.sigmoid(jnp.dot(h3, wpg_ref[...], preferred_element_type=F32) + bpg_ref[...])
    ple = jnp.dot(p_ref[...].astype(BF16), wple_ref[...], preferred_element_type=F32)
    out = acc + ple * pg
    if final_norm:
        out = _rms_norm(out, gf_ref[...])
    o_ref[...] = out


def _mix(x2, oa, ob, gates, p2, w_a, w_b, w_o, g2, w_up, conv_w, conv_b, w_down, g3, w_ple, w_pg,
         b_pg, gf, *, seq, tm, final_norm):
    m, d = x2.shape
    d_ff = w_down.shape[0]
    assert conv_w.shape[0] == CONV_W, "the kernel body unrolls the causal conv for this width"
    body = functools.partial(_mix_kernel, blocks_per_seq=seq // tm, final_norm=final_norm)
    row = lambda t: pl.BlockSpec((tm, t.shape[1]), lambda i: (i, 0))
    weights = (w_a, w_b, w_o, g2, w_up, conv_w, conv_b, w_down, g3, w_ple, w_pg, b_pg, gf)
    return pl.pallas_call(
        body,
        grid=(m // tm,),
        in_specs=[row(x2), row(oa), row(ob), row(gates), row(p2)]
                 + [_resident(w.shape) for w in weights],
        out_specs=row(x2),
        out_shape=jax.ShapeDtypeStruct((m, d), F32),
        scratch_shapes=[pltpu.VMEM((SUBLANES, d_ff), F32)],
        compiler_params=pltpu.CompilerParams(
            dimension_semantics=("arbitrary",), vmem_limit_bytes=VMEM_LIMIT_BYTES),
        name="mix",
    )(x2, oa, ob, gates, p2, *weights)


def kernel(x, p, ln1_g, w_in, mix_mu, w0, w2, a0, a2, g2, k_k, k_a, r_k, lnx_w, lnx_b, rel_bias, gate_b, w_br_rwkv, w_br_attn, w_o, ln2_g, w_ffn_up, conv_w, conv_b, w_ffn_down, ln3_g, w_ple, w_pg, b_pg, lnf_g):
    nb, seq, d = x.shape
    depth = w_in.shape[0]
    rwkv_width = w_br_rwkv.shape[1]
    attn_width = w_br_attn.shape[1]
    rwkv_cols = 3 * rwkv_width + DECAY_RANK + ICLR_RANK + GATE_RANK
    attn_cols = 3 * attn_width
    assert DECAY_RANK + ICLR_RANK == LANES and GATE_RANK == LANES
    attn_rows = min(seq, 2048)
    assert seq % 512 == 0 and seq % attn_rows == 0 and seq >= BAND
    row = lambda t: t.reshape(1, -1)
    m = nb * seq
    x2 = x.reshape(m, d)

    for i in range(depth):
        last = i == depth - 1
        zs, qkv, gates = _inproj(
            x2, row(ln1_g[i]), w_in[i].astype(BF16), row(mix_mu[i]), row(gate_b[i]),
            seq=seq, rwkv_cols=rwkv_cols, attn_cols=attn_cols, tm=512)

        w2p = jnp.concatenate([w2[i], jnp.zeros_like(a2[i])], axis=0)
        a2p = jnp.concatenate([jnp.zeros_like(w2[i]), a2[i]], axis=0)
        o_a = _rwkv(zs.reshape(nb, seq, rwkv_cols), row(w0[i]), w2p, row(a0[i]), a2p, g2[i],
                    row(k_k[i]), row(k_a[i]), row(r_k[i]), row(lnx_w[i]), row(lnx_b[i]),
                    width=rwkv_width, tb=512)

        o_b = _attention(qkv.reshape(-1, nb, seq, LANES), _band_bias(rel_bias[i]),
                         width=attn_width, tq=attn_rows)

        x2 = _mix(x2, o_a.reshape(m, rwkv_width), o_b.reshape(m, attn_width), gates,
                  p[i].reshape(m, -1), w_br_rwkv[i].astype(BF16), w_br_attn[i].astype(BF16),
                  w_o[i].astype(BF16), row(ln2_g[i]), w_ffn_up[i].astype(BF16), conv_w[i],
                  row(conv_b[i]), w_ffn_down[i].astype(BF16), row(ln3_g[i]), w_ple[i].astype(BF16),
                  w_pg[i].astype(BF16), row(b_pg[i]), row(lnf_g), seq=seq, tm=512, final_norm=last)
    return x2.reshape(nb, seq, d)
```
